```python
import math
import jax, jax.numpy as jnp
from jax import lax
import numpy as np

D_MODEL = 4096
BATCH = 2
SEQ = 8192
DEPTH = 1

HEAD_DIM = 128
A_HEADS = 16
A_KV_HEADS = 4
A_GROUP = A_HEADS // A_KV_HEADS
WINDOW = 128
A_WIDTH = A_HEADS * HEAD_DIM
B_HEADS = 16
B_WIDTH = B_HEADS * HEAD_DIM
Q_RANK = 1024
KV_RANK = 512
IDX_HEADS = 16
IDX_DIM = 128
TOPK_KEYS = 256
Q_BLOCK = 128
N_BUCKETS = 32
MAX_DISTANCE = 128
MEM_LEN = 256
MEM_HEADS = 4
MEM_WIDTH = MEM_HEADS * HEAD_DIM
N_GROUPS = 8
EXPERTS_PER_GROUP = 8
N_EXPERTS = N_GROUPS * EXPERTS_PER_GROUP
EXPERT_FF = 768
EXPERT_TOPK = 2
MOE_BLOCK = 128
EPS = 1e-6
IN_SIZES = (A_WIDTH, A_KV_HEADS * HEAD_DIM, A_KV_HEADS * HEAD_DIM, Q_RANK, KV_RANK, IDX_DIM, IDX_HEADS, D_MODEL, D_MODEL)
IN_WIDTH = A_WIDTH + 2 * A_KV_HEADS * HEAD_DIM + Q_RANK + KV_RANK + IDX_DIM + IDX_HEADS + 2 * D_MODEL

kernel_name = "hybrid_swa_dsa_hmoe_block"


def rmsnorm(x, g):
    xf = x.astype(jnp.float32)
    y = xf * lax.rsqrt(jnp.mean(xf * xf, axis=-1, keepdims=True) + EPS)
    return (y * g.astype(jnp.float32)).astype(x.dtype)


def rel_bucket(dist):
    max_exact = N_BUCKETS // 2
    d = jnp.maximum(dist, 0)
    df = jnp.maximum(d, 1).astype(jnp.float32)
    large = max_exact + (jnp.log(df / max_exact) / math.log(MAX_DISTANCE / max_exact) * (N_BUCKETS - max_exact)).astype(jnp.int32)
    large = jnp.minimum(large, N_BUCKETS - 1)
    return jnp.where(d < max_exact, d, large)


def swa_sink_attention(q, k, v, sink, bias_table):
    bsz, s = q.shape[0], q.shape[1]
    nb = s // WINDOW
    q = q.reshape(bsz, nb, WINDOW, A_KV_HEADS, A_GROUP, HEAD_DIM)
    k = k.reshape(bsz, nb, WINDOW, A_KV_HEADS, HEAD_DIM)
    v = v.reshape(bsz, nb, WINDOW, A_KV_HEADS, HEAD_DIM)

    def band(t):
        prev = jnp.pad(t[:, :-1], ((0, 0), (1, 0), (0, 0), (0, 0), (0, 0)))
        return jnp.concatenate([prev, t], axis=2)

    kb, vb = band(k), band(v)
    logits = jnp.einsum('bnqkgd,bnskd->bnkgqs', q, kb, preferred_element_type=jnp.float32) * (HEAD_DIM ** -0.5)
    qi = jnp.arange(WINDOW, dtype=jnp.int32)[:, None]
    sj = jnp.arange(2 * WINDOW, dtype=jnp.int32)[None, :]
    dist = qi + WINDOW - sj
    bias = bias_table[rel_bucket(dist)]
    bias = jnp.transpose(bias, (2, 0, 1)).reshape(A_KV_HEADS, A_GROUP, WINDOW, 2 * WINDOW).astype(jnp.float32)
    key_pos = (jnp.arange(nb, dtype=jnp.int32) * WINDOW - WINDOW)[:, None, None] + sj[None]
    valid = (dist >= 0) & (dist < WINDOW) & (key_pos >= 0)
    logits = jnp.where(valid[None, :, None, None], logits + bias, -jnp.inf)
    sink_l = jnp.broadcast_to(sink.reshape(A_KV_HEADS, A_GROUP, 1, 1).astype(jnp.float32), logits.shape[:-1] + (1,))
    p = jax.nn.softmax(jnp.concatenate([logits, sink_l], axis=-1), axis=-1)[..., :-1]
    out = jnp.einsum('bnkgqs,bnskd->bnqkgd', p.astype(vb.dtype), vb)
    return out.reshape(bsz, s, A_WIDTH)


def dsa_mla_attention(q_b, q_idx, w_idx, k_idx, c_kv, w_uk, w_uv, bias_table):
    bsz, s = q_b.shape[0], q_b.shape[1]
    nb = s // Q_BLOCK
    n_keep = min(TOPK_KEYS, s // 4)
    key_pos = jnp.arange(s, dtype=jnp.int32)

    def to_blocks(t):
        return jnp.moveaxis(t.reshape((bsz, nb, Q_BLOCK) + t.shape[2:]), 1, 0)

    def block(args):
        qb, qi, wi, start = args
        t = start + jnp.arange(Q_BLOCK, dtype=jnp.int32)
        rel = jax.nn.relu(jnp.einsum('bqhd,bsd->bqhs', qi, k_idx, preferred_element_type=jnp.float32) * (IDX_DIM ** -0.5))
        score = jnp.einsum('bqh,bqhs->bqs', wi.astype(jnp.float32), rel) * (IDX_HEADS ** -0.5)
        score = jnp.where((key_pos[None, :] <= t[:, None])[None], score, -jnp.inf)
        _, idx = lax.top_k(score, n_keep)
        c_sel = jax.vmap(lambda c, i: c[i])(c_kv, idx)
        q_lat = jnp.einsum('bqhd,rhd->bqhr', qb, w_uk)
        logits = jnp.einsum('bqhr,bqkr->bqhk', q_lat, c_sel, preferred_element_type=jnp.float32) * (HEAD_DIM ** -0.5)
        dist = t[None, :, None] - idx
        bias = jnp.moveaxis(bias_table[rel_bucket(dist)], -1, 2).astype(jnp.float32)
        logits = jnp.where((dist >= 0)[:, :, None, :], logits + bias, -jnp.inf)
        p = jax.nn.softmax(logits, axis=-1).astype(c_sel.dtype)
        o_lat = jnp.einsum('bqhk,bqkr->bqhr', p, c_sel)
        o = jnp.einsum('bqhr,rhd->bqhd', o_lat, w_uv)
        return o.reshape(bsz, Q_BLOCK, B_WIDTH)

    starts = jnp.arange(nb, dtype=jnp.int32) * Q_BLOCK
    out = lax.map(block, (to_blocks(q_b), to_blocks(q_idx), to_blocks(w_idx), starts))
    return jnp.moveaxis(out, 0, 1).reshape(bsz, s, B_WIDTH)


def memory_cross_attention(u, m, w_qm, w_km, w_vm, w_om):
    bsz, s = u.shape[0], u.shape[1]
    n_mem = m.shape[1]
    q = (u @ w_qm).reshape(bsz, s, MEM_HEADS, HEAD_DIM)
    k = (m @ w_km).reshape(bsz, n_mem, MEM_HEADS, HEAD_DIM)
    v = (m @ w_vm).reshape(bsz, n_mem, MEM_HEADS, HEAD_DIM)
    logits = jnp.einsum('bshd,bmhd->bhsm', q, k, preferred_element_type=jnp.float32) * (HEAD_DIM ** -0.5)
    p = jax.nn.softmax(logits, axis=-1).astype(v.dtype)
    o = jnp.einsum('bhsm,bmhd->bshd', p, v).reshape(bsz, s, MEM_WIDTH)
    return o @ w_om


def hierarchical_moe(u, w_grp, b_grp, w_exp, b_exp, w1, w3, w2):
    bsz, s, d = u.shape
    n = bsz * s
    ut = u.reshape(n, d)
    grp_logits = (ut @ w_grp).astype(jnp.float32) + b_grp.astype(jnp.float32)
    grp_p = jax.nn.softmax(grp_logits, axis=-1)
    g_top = jnp.argmax(grp_logits, axis=-1).astype(jnp.int32)
    g_gate = jnp.take_along_axis(grp_p, g_top[:, None], axis=1)
    exp_logits = ((ut @ w_exp).astype(jnp.float32) + b_exp.astype(jnp.float32)).reshape(n, N_GROUPS, EXPERTS_PER_GROUP)
    in_grp = jnp.take_along_axis(exp_logits, g_top[:, None, None], axis=1)[:, 0]
    top_v, top_j = lax.top_k(in_grp, EXPERT_TOPK)
    gate = g_gate * jax.nn.softmax(top_v, axis=-1)
    eid = (g_top[:, None] * EXPERTS_PER_GROUP + top_j).reshape(-1).astype(jnp.int32)
    w_f = gate.reshape(-1)
    n_assign = n * EXPERT_TOPK
    tok_f = jnp.repeat(jnp.arange(n, dtype=jnp.int32), EXPERT_TOPK)
    order = jnp.argsort(eid)
    se = eid[order]
    counts = jnp.bincount(eid, length=N_EXPERTS).astype(jnp.int32)
    padded = ((counts + MOE_BLOCK - 1) // MOE_BLOCK) * MOE_BLOCK
    pad_end = jnp.cumsum(padded)
    pad_start = pad_end - padded
    start = jnp.cumsum(counts) - counts
    rank = jnp.arange(n_assign, dtype=jnp.int32) - start[se]
    dest = pad_start[se] + rank
    n_slots = n_assign + N_EXPERTS * MOE_BLOCK
    n_blk = n_slots // MOE_BLOCK
    slot_tok = jnp.full((n_slots,), n, jnp.int32).at[dest].set(tok_f[order])
    slot_w = jnp.zeros((n_slots,), jnp.float32).at[dest].set(w_f[order])
    blk_e = jnp.minimum(jnp.searchsorted(pad_end, jnp.arange(n_blk, dtype=jnp.int32) * MOE_BLOCK, side='right'), N_EXPERTS - 1).astype(jnp.int32)
    u_ext = jnp.concatenate([ut, jnp.zeros((1, d), ut.dtype)], axis=0)

    def run(args):
        tok, e = args
        h = u_ext[tok]
        return (jax.nn.silu(h @ w1[e]) * (h @ w3[e])) @ w2[e]

    y = lax.map(run, (slot_tok.reshape(n_blk, MOE_BLOCK), blk_e)).reshape(n_slots, d)
    out = jnp.zeros((n + 1, d), y.dtype).at[slot_tok].add(y * slot_w[:, None].astype(y.dtype))[:n]
    return out.reshape(bsz, s, d).astype(u.dtype)


def hybrid_layer(x, mem, rel_bias, g_mix, w_in, g_cq, w_uq, w_qidx, g_ckv, w_uk, w_uv, g_kidx, sink_a,
                 w_pa, w_pb, w_out, g_xattn, g_mem, w_qm, w_km, w_vm, w_om, g_ffn,
                 w_grp, b_grp, w_exp, b_exp, w_e1, w_e3, w_e2):
    bsz, s, _ = x.shape
    u = rmsnorm(x, g_mix)
    proj = u @ w_in
    split_pts = [int(p) for p in np.cumsum(IN_SIZES)[:-1]]
    q_a, k_a, v_a, c_q, c_kv, k_i, w_i, gate_a, gate_b = jnp.split(proj, split_pts, axis=-1)
    o_a = swa_sink_attention(q_a.reshape(bsz, s, A_HEADS, HEAD_DIM),
                             k_a.reshape(bsz, s, A_KV_HEADS, HEAD_DIM),
                             v_a.reshape(bsz, s, A_KV_HEADS, HEAD_DIM),
                             sink_a, rel_bias[:, :A_HEADS])
    c_q = rmsnorm(c_q, g_cq)
    c_kv = rmsnorm(c_kv, g_ckv)
    k_i = rmsnorm(k_i, g_kidx)
    q_b = (c_q @ w_uq).reshape(bsz, s, B_HEADS, HEAD_DIM)
    q_i = (c_q @ w_qidx).reshape(bsz, s, IDX_HEADS, IDX_DIM)
    o_b = dsa_mla_attention(q_b, q_i, w_i, k_i, c_kv, w_uk, w_uv, rel_bias[:, A_HEADS:])
    y = jax.nn.sigmoid(gate_a) * (o_a @ w_pa) + jax.nn.sigmoid(gate_b) * (o_b @ w_pb)
    x = x + y @ w_out
    x = x + memory_cross_attention(rmsnorm(x, g_xattn), rmsnorm(mem, g_mem), w_qm, w_km, w_vm, w_om)
    x = x + hierarchical_moe(rmsnorm(x, g_ffn), w_grp, b_grp, w_exp, b_exp, w_e1, w_e3, w_e2)
    return x


def setup_inputs(seed: int = 0) -> dict:
    key = jax.random.key(seed)
    ks = jax.random.split(key, 32)
    f32 = jnp.float32
    L = DEPTH

    def nrm(k, shape, scale):
        return jax.random.normal(k, shape, f32) * scale

    def gain(k, shape):
        return 1.0 + 0.02 * jax.random.normal(k, shape, f32)

    return {
        "x": nrm(ks[0], (BATCH, SEQ, D_MODEL), 1.0),
        "mem": nrm(ks[1], (BATCH, MEM_LEN, D_MODEL), 1.0),
        "rel_bias": nrm(ks[2], (N_BUCKETS, A_HEADS + B_HEADS), 0.5),
        "g_mix": gain(ks[3], (L, D_MODEL)),
        "w_in": nrm(ks[4], (L, D_MODEL, IN_WIDTH), D_MODEL ** -0.5),
        "g_cq": gain(ks[5], (L, Q_RANK)),
        "w_uq": nrm(ks[6], (L, Q_RANK, B_WIDTH), Q_RANK ** -0.5),
        "w_qidx": nrm(ks[7], (L, Q_RANK, IDX_HEADS * IDX_DIM), Q_RANK ** -0.5),
        "g_ckv": gain(ks[8], (L, KV_RANK)),
        "w_uk": nrm(ks[9], (L, KV_RANK, B_HEADS, HEAD_DIM), KV_RANK ** -0.5),
        "w_uv": nrm(ks[10], (L, KV_RANK, B_HEADS, HEAD_DIM), KV_RANK ** -0.5),
        "g_kidx": gain(ks[11], (L, IDX_DIM)),
        "sink_a": nrm(ks[12], (L, A_HEADS), 0.5),
        "w_pa": nrm(ks[13], (L, A_WIDTH, D_MODEL), A_WIDTH ** -0.5),
        "w_pb": nrm(ks[14], (L, B_WIDTH, D_MODEL), B_WIDTH ** -0.5),
        "w_out": nrm(ks[15], (L, D_MODEL, D_MODEL), D_MODEL ** -0.5),
        "g_xattn": gain(ks[16], (L, D_MODEL)),
        "g_mem": gain(ks[17], (L, D_MODEL)),
        "w_qm": nrm(ks[18], (L, D_MODEL, MEM_WIDTH), D_MODEL ** -0.5),
        "w_km": nrm(ks[19], (L, D_MODEL, MEM_WIDTH), D_MODEL ** -0.5),
        "w_vm": nrm(ks[20], (L, D_MODEL, MEM_WIDTH), D_MODEL ** -0.5),
        "w_om": nrm(ks[21], (L, MEM_WIDTH, D_MODEL), MEM_WIDTH ** -0.5),
        "g_ffn": gain(ks[22], (L, D_MODEL)),
        "w_grp": nrm(ks[23], (L, D_MODEL, N_GROUPS), D_MODEL ** -0.5),
        "b_grp": nrm(ks[24], (L, N_GROUPS), 0.01),
        "w_exp": nrm(ks[25], (L, D_MODEL, N_EXPERTS), D_MODEL ** -0.5),
        "b_exp": nrm(ks[26], (L, N_EXPERTS), 0.01),
        "w_e1": nrm(ks[27], (L, N_EXPERTS, D_MODEL, EXPERT_FF), D_MODEL ** -0.5),
        "w_e3": nrm(ks[28], (L, N_EXPERTS, D_MODEL, EXPERT_FF), D_MODEL ** -0.5),
        "w_e2": nrm(ks[29], (L, N_EXPERTS, EXPERT_FF, D_MODEL), EXPERT_FF ** -0.5),
        "g_final": gain(ks[30], (D_MODEL,)),
    }


def reference(x, mem, rel_bias, g_mix, w_in, g_cq, w_uq, w_qidx, g_ckv, w_uk, w_uv, g_kidx, sink_a,
              w_pa, w_pb, w_out, g_xattn, g_mem, w_qm, w_km, w_vm, w_om, g_ffn,
              w_grp, b_grp, w_exp, b_exp, w_e1, w_e3, w_e2, g_final):
    for l in range(DEPTH):
        x = hybrid_layer(x, mem, rel_bias, g_mix[l], w_in[l], g_cq[l], w_uq[l], w_qidx[l], g_ckv[l],
                         w_uk[l], w_uv[l], g_kidx[l], sink_a[l], w_pa[l], w_pb[l], w_out[l],
                         g_xattn[l], g_mem[l], w_qm[l], w_km[l], w_vm[l], w_om[l], g_ffn[l],
                         w_grp[l], b_grp[l], w_exp[l], b_exp[l], w_e1[l], w_e3[l], w_e2[l])
    return rmsnorm(x, g_final)
```

```python
import functools
import math

import numpy as np
import jax
import jax.numpy as jnp
from jax import lax
from jax.experimental import pallas as pl
from jax.experimental.pallas import tpu as pltpu

F32 = jnp.float32
BF16 = jnp.bfloat16

HEAD_DIM = 128
A_HEADS = 16
A_KV_HEADS = 4
A_GROUP = A_HEADS // A_KV_HEADS
WINDOW = 128
A_WIDTH = A_HEADS * HEAD_DIM
KV_WIDTH = A_KV_HEADS * HEAD_DIM
B_HEADS = 16
B_WIDTH = B_HEADS * HEAD_DIM
Q_RANK = 1024
KV_RANK = 512
IDX_HEADS = 16
IDX_DIM = 128
TOPK_KEYS = 256
Q_BLOCK = 128
N_BUCKETS = 32
MAX_DISTANCE = 128
MEM_HEADS = 4
MEM_WIDTH = MEM_HEADS * HEAD_DIM
N_GROUPS = 8
EXPERTS_PER_GROUP = 8
N_EXPERTS = N_GROUPS * EXPERTS_PER_GROUP
EXPERT_FF = 768
EXPERT_TOPK = 2
EPS = 1e-6

LANE = 128
LATENT_WIDTH = 2048
W_IDX_COL = Q_RANK + KV_RANK + IDX_DIM
MOE_ROWS = 512
FF_CHUNK = 256
ROUTER_WIDTH = 128
INT_MIN = -(2 ** 31)
KEY_NEG_INF = int(np.array([0xFF800000 ^ 0x7FFFFFFF], np.uint32).view(np.int32)[0])
NEG_BIG = -1e30
NT_DIMS = (((1,), (1,)), ((), ()))


def _params(semantics, vmem_mb):
    return pltpu.CompilerParams(dimension_semantics=semantics, vmem_limit_bytes=int(vmem_mb * 2 ** 20))


def _rms(xf, g):
    return xf * lax.rsqrt(jnp.mean(xf * xf, axis=-1, keepdims=True) + EPS) * g


def _resident(block, index_map):
    return pl.BlockSpec(block, index_map, pipeline_mode=pl.Buffered(1))


def _rmsnorm_kernel(x_ref, g_ref, o_ref):
    o_ref[...] = _rms(x_ref[...], g_ref[...]).astype(o_ref.dtype)


def rmsnorm_cast(x, g, tm=512):
    m, k = x.shape
    return pl.pallas_call(
        _rmsnorm_kernel,
        out_shape=jax.ShapeDtypeStruct((m, k), BF16),
        grid=(m // tm,),
        in_specs=[pl.BlockSpec((tm, k), lambda i: (i, 0)), pl.BlockSpec((1, k), lambda i: (0, 0))],
        out_specs=pl.BlockSpec((tm, k), lambda i: (i, 0)),
        compiler_params=_params(("parallel",), 40),
        name="rmsnorm_cast",
    )(x, g.reshape(1, k))


def _mm_kernel(a_ref, w_ref, o_ref):
    o_ref[...] = jnp.dot(a_ref[...], w_ref[...], preferred_element_type=F32).astype(o_ref.dtype)


def matmul(a, w, out_dtype, tm=1024, tn=512):
    m, k = a.shape
    n = w.shape[1]
    return pl.pallas_call(
        _mm_kernel,
        out_shape=jax.ShapeDtypeStruct((m, n), out_dtype),
        grid=(m // tm, n // tn),
        in_specs=[pl.BlockSpec((tm, k), lambda i, j: (i, 0)), pl.BlockSpec((k, tn), lambda i, j: (0, j))],
        out_specs=pl.BlockSpec((tm, tn), lambda i, j: (i, j)),
        compiler_params=_params(("parallel", "arbitrary"), 48),
        name="matmul",
    )(a, w)


def _mm_res_kernel(a_ref, w_ref, r_ref, o_ref):
    o_ref[...] = r_ref[...] + jnp.dot(a_ref[...], w_ref[...], preferred_element_type=F32)


def matmul_residual(a, w, res, tm=1024, tn=512):
    m, k = a.shape
    n = w.shape[1]
    return pl.pallas_call(
        _mm_res_kernel,
        out_shape=jax.ShapeDtypeStruct((m, n), F32),
        grid=(m // tm, n // tn),
        in_specs=[pl.BlockSpec((tm, k), lambda i, j: (i, 0)), pl.BlockSpec((k, tn), lambda i, j: (0, j)),
                  pl.BlockSpec((tm, tn), lambda i, j: (i, j))],
        out_specs=pl.BlockSpec((tm, tn), lambda i, j: (i, j)),
        compiler_params=_params(("parallel", "arbitrary"), 48),
        name="matmul_residual",
    )(a, w, res)


def _rms_mm_heads_kernel(x_ref, g_ref, w_ref, o_ref, u_ref):
    @pl.when(pl.program_id(1) == 0)
    def _():
        u_ref[...] = _rms(x_ref[...], g_ref[...]).astype(BF16)

    r = jnp.dot(u_ref[...], w_ref[...], preferred_element_type=F32)
    for hh in range(o_ref.shape[0]):
        o_ref[hh] = r[:, hh * LANE:(hh + 1) * LANE].astype(o_ref.dtype)


def rms_matmul_heads(x, g, w, tm=1024, tn=512):
    m = x.shape[0]
    k, n = w.shape
    hb = tn // LANE
    return pl.pallas_call(
        _rms_mm_heads_kernel,
        out_shape=jax.ShapeDtypeStruct((n // LANE, m, LANE), BF16),
        grid=(m // tm, n // tn),
        in_specs=[pl.BlockSpec((tm, k), lambda i, j: (i, 0)), pl.BlockSpec((1, k), lambda i, j: (0, 0)),
                  pl.BlockSpec((k, tn), lambda i, j: (0, j))],
        out_specs=pl.BlockSpec((hb, tm, LANE), lambda i, j: (j, i, 0)),
        scratch_shapes=[pltpu.VMEM((tm, k), BF16)],
        compiler_params=_params(("parallel", "arbitrary"), 40),
        name="rms_matmul_heads",
    )(x, g.reshape(1, k), w)


def _swa_kernel(sink_ref, q_ref, kp_ref, kc_ref, vp_ref, vc_ref, bias_ref, o_ref):
    n = pl.program_id(1)
    kb = jnp.concatenate([kp_ref[...], kc_ref[...]], axis=0)
    vb = jnp.concatenate([vp_ref[...], vc_ref[...]], axis=0)
    col = lax.broadcasted_iota(jnp.int32, (1, 2 * WINDOW), 1)
    no_prev = jnp.where((col < WINDOW) & (n == 0), -jnp.inf, 0.0)
    for kh in range(A_KV_HEADS):
        heads = [kh * A_GROUP + g for g in range(A_GROUP)]
        qs = jnp.concatenate([q_ref[:, h * HEAD_DIM:(h + 1) * HEAD_DIM] for h in heads], axis=0)
        k = kb[:, kh * HEAD_DIM:(kh + 1) * HEAD_DIM]
        v = vb[:, kh * HEAD_DIM:(kh + 1) * HEAD_DIM]
        lg = lax.dot_general(qs, k, NT_DIMS, preferred_element_type=F32) * (HEAD_DIM ** -0.5)
        lg = lg + bias_ref[kh] + no_prev
        sk = jnp.concatenate([jnp.full((WINDOW, 1), sink_ref[h], F32) for h in heads], axis=0)
        mx = jnp.maximum(jnp.max(lg, axis=-1, keepdims=True), sk)
        p = jnp.exp(lg - mx)
        den = jnp.sum(p, axis=-1, keepdims=True) + jnp.exp(sk - mx)
        o = jnp.dot((p * (1.0 / den)).astype(BF16), v, preferred_element_type=F32)
        for g, h in enumerate(heads):
            o_ref[:, h * HEAD_DIM:(h + 1) * HEAD_DIM] = o[g * WINDOW:(g + 1) * WINDOW].astype(o_ref.dtype)


def swa_attention(qkv, sink, bias, bsz, s):
    nb = s // WINDOW
    kcol = A_WIDTH // KV_WIDTH
    cur = lambda b, n: b * nb + n
    prev = lambda b, n: b * nb + jnp.maximum(n - 1, 0)
    return pl.pallas_call(
        _swa_kernel,
        out_shape=jax.ShapeDtypeStruct((bsz * s, A_WIDTH), BF16),
        grid=(bsz, nb),
        in_specs=[
            pl.BlockSpec(memory_space=pltpu.SMEM),
            pl.BlockSpec((WINDOW, A_WIDTH), lambda b, n: (cur(b, n), 0)),
            pl.BlockSpec((WINDOW, KV_WIDTH), lambda b, n: (prev(b, n), kcol)),
            pl.BlockSpec((WINDOW, KV_WIDTH), lambda b, n: (cur(b, n), kcol)),
            pl.BlockSpec((WINDOW, KV_WIDTH), lambda b, n: (prev(b, n), kcol + 1)),
            pl.BlockSpec((WINDOW, KV_WIDTH), lambda b, n: (cur(b, n), kcol + 1)),
            pl.BlockSpec((A_KV_HEADS, A_GROUP * WINDOW, 2 * WINDOW), lambda b, n: (0, 0, 0)),
        ],
        out_specs=pl.BlockSpec((WINDOW, A_WIDTH), lambda b, n: (cur(b, n), 0)),
        compiler_params=_params(("parallel", "arbitrary"), 32),
        name="swa_attention",
    )(sink, qkv, qkv, qkv, qkv, qkv, bias)


def _latent_norm_kernel(x_ref, gkv_ref, gk_ref, ckv_ref, ckvt_ref, kidx_ref):
    c = _rms(x_ref[:, :KV_RANK], gkv_ref[...])
    ckv_ref[...] = c.astype(BF16)
    ckvt_ref[...] = c.T.astype(BF16)
    kidx_ref[...] = _rms(x_ref[:, KV_RANK:KV_RANK + IDX_DIM], gk_ref[...]).astype(BF16)


def latent_norm(latent, g_ckv, g_kidx, bsz, s, tm=512):
    per = s // tm
    half = LATENT_WIDTH // 2
    return pl.pallas_call(
        _latent_norm_kernel,
        out_shape=(jax.ShapeDtypeStruct((bsz, s, KV_RANK), BF16), jax.ShapeDtypeStruct((bsz, KV_RANK, s), BF16),
                   jax.ShapeDtypeStruct((bsz, s, IDX_DIM), BF16)),
        grid=(bsz * per,),
        in_specs=[pl.BlockSpec((tm, half), lambda i: (i, 1)), pl.BlockSpec((1, KV_RANK), lambda i: (0, 0)),
                  pl.BlockSpec((1, IDX_DIM), lambda i: (0, 0))],
        out_specs=(pl.BlockSpec((None, tm, KV_RANK), lambda i: (i // per, i % per, 0)),
                   pl.BlockSpec((None, KV_RANK, tm), lambda i: (i // per, 0, i % per)),
                   pl.BlockSpec((None, tm, IDX_DIM), lambda i: (i // per, i % per, 0))),
        compiler_params=_params(("parallel",), 32),
        name="latent_norm",
    )(latent, g_ckv.reshape(1, KV_RANK), g_kidx.reshape(1, IDX_DIM))


P1_CHUNK = 512
P2_CHUNK = 256


def _sortable_key(x):
    bits = lax.bitcast_convert_type(x, jnp.int32)
    return bits ^ ((bits >> 31) & 0x7FFFFFFF)


def _dsa_kernel(qb_ref, qi_ref, w_ref, kidx_ref, ckv_ref, ckvt_ref, wuk_ref, wuv_ref, tab_ref, tri_ref, o_ref,
                key_ref, qlat_ref, acc_ref, m_ref, l_ref, alpha_ref, p_ref):
    n = pl.program_id(1)
    nheads = B_HEADS
    lane_q = lax.broadcasted_iota(jnp.int32, (1, Q_BLOCK), 1)
    t_row = n * Q_BLOCK + lane_q

    for h in range(nheads):
        ql = jnp.dot(qb_ref[h], wuk_ref[h], preferred_element_type=F32) * (HEAD_DIM ** -0.5)
        qlat_ref[h * Q_BLOCK:(h + 1) * Q_BLOCK, :] = ql.astype(BF16)

    w_t = w_ref[...].T

    n_p1 = (n * Q_BLOCK + Q_BLOCK + P1_CHUNK - 1) // P1_CHUNK

    def p1_body(c, carry):
        start = pl.multiple_of(c * P1_CHUNK, P1_CHUNK)
        kc = kidx_ref[pl.ds(start, P1_CHUNK), :]
        score = jnp.zeros((P1_CHUNK, Q_BLOCK), F32)
        hg = 4
        for g0 in range(0, IDX_HEADS, hg):
            q4 = qi_ref[g0:g0 + hg].reshape(hg * Q_BLOCK, IDX_DIM)
            st = lax.dot_general(kc, q4, NT_DIMS, preferred_element_type=F32) * (IDX_DIM ** -0.5)
            for j in range(hg):
                h = g0 + j
                score = score + jnp.maximum(st[:, j * Q_BLOCK:(j + 1) * Q_BLOCK], 0.0) * w_t[h:h + 1, :]
        score = score * (IDX_HEADS ** -0.5)
        s_idx = start + lax.broadcasted_iota(jnp.int32, (P1_CHUNK, Q_BLOCK), 0)
        score = jnp.where(s_idx <= t_row, score, -jnp.inf)
        key_ref[pl.ds(start, P1_CHUNK), :] = _sortable_key(score)
        return carry

    lax.fori_loop(0, n_p1, p1_body, 0)

    def count(pred_fn):
        def body(c, cnt8):
            blk = key_ref[pl.ds(pl.multiple_of(c * P1_CHUNK, P1_CHUNK), P1_CHUNK), :]
            hit = jnp.where(pred_fn(blk), 1, 0).astype(jnp.int32)
            return cnt8 + hit.reshape(P1_CHUNK // 8, 8, Q_BLOCK).sum(axis=0)

        cnt8 = lax.fori_loop(0, n_p1, body, jnp.zeros((8, Q_BLOCK), jnp.int32))
        return cnt8.sum(axis=0, keepdims=True)

    def bit_body(i, thr):
        cand = thr + jnp.left_shift(jnp.int32(1), 31 - i)
        return jnp.where(count(lambda blk: blk >= cand) >= TOPK_KEYS, cand, thr)

    thr = lax.fori_loop(0, 32, bit_body, jnp.full((1, Q_BLOCK), INT_MIN, jnp.int32))
    need = (TOPK_KEYS - count(lambda blk: blk > thr)).astype(F32)

    m_ref[...] = jnp.full(m_ref.shape, NEG_BIG, F32)
    l_ref[...] = jnp.zeros(l_ref.shape, F32)
    acc_ref[...] = jnp.zeros(acc_ref.shape, F32)

    def chunk(c, eq_seen, near):
        start = pl.multiple_of(c * P2_CHUNK, P2_CHUNK)
        u = key_ref[pl.ds(start, P2_CHUNK), :]
        eq = u == thr
        pref = jnp.dot(tri_ref[...], jnp.where(eq, 1.0, 0.0).astype(BF16), preferred_element_type=F32)
        sel = ((u > thr) | (eq & (eq_seen + pref <= need))) & (u != KEY_NEG_INF)
        mask = jnp.where(sel, 0.0, -jnp.inf)
        eq_seen = eq_seen + pref[P2_CHUNK - 1:P2_CHUNK, :]
        lg = lax.dot_general(ckv_ref[pl.ds(start, P2_CHUNK), :], qlat_ref[...], NT_DIMS,
                             preferred_element_type=F32)
        if near:
            halves = []
            for half in range(P2_CHUNK // Q_BLOCK):
                blk = c * (P2_CHUNK // Q_BLOCK) + half
                halves.append(jnp.where(blk == n, tab_ref[1], jnp.where(blk == n - 1, tab_ref[0], 0.0)))
            bias = jnp.concatenate(halves, axis=0)
        for h in range(nheads):
            hs = slice(h * Q_BLOCK, (h + 1) * Q_BLOCK)
            lh = lg[:, hs] + mask
            if near:
                lh = lh + bias[:, hs]
            m_old = m_ref[:, hs]
            m_new = jnp.maximum(m_old, jnp.max(lh, axis=0, keepdims=True))
            alpha = jnp.exp(m_old - m_new)
            p = jnp.exp(lh - m_new)
            l_ref[:, hs] = alpha * l_ref[:, hs] + jnp.sum(p, axis=0, keepdims=True)
            m_ref[:, hs] = m_new
            alpha_ref[:, hs] = alpha
            p_ref[:, hs] = p.astype(BF16)
        pv = jnp.dot(ckvt_ref[:, pl.ds(start, P2_CHUNK)], p_ref[...], preferred_element_type=F32)
        acc_ref[...] = acc_ref[...] * alpha_ref[...] + pv
        return eq_seen

    per = P2_CHUNK // Q_BLOCK
    c_end = n // per + 1
    c_near = jnp.maximum(n - 1, 0) // per
    eq_seen = lax.fori_loop(0, c_near, functools.partial(chunk, near=False), jnp.zeros((1, Q_BLOCK), F32))
    lax.fori_loop(c_near, c_end, functools.partial(chunk, near=True), eq_seen)

    for h in range(nheads):
        hs = slice(h * Q_BLOCK, (h + 1) * Q_BLOCK)
        o_lat = (acc_ref[:, hs] * (1.0 / l_ref[:, hs])).T
        o = jnp.dot(o_lat.astype(BF16), wuv_ref[h], preferred_element_type=F32)
        o_ref[:, h * HEAD_DIM:(h + 1) * HEAD_DIM] = o.astype(o_ref.dtype)


def dsa_attention(qbi, latent, kidx, ckv, ckvt, wuk_t, wuv_h, tab, tri, bsz, s):
    nb = s // Q_BLOCK
    row = lambda b, n: b * nb + n
    hq = B_HEADS * Q_BLOCK
    return pl.pallas_call(
        _dsa_kernel,
        out_shape=jax.ShapeDtypeStruct((bsz * s, B_WIDTH), BF16),
        grid=(bsz, nb),
        in_specs=[
            pl.BlockSpec((B_HEADS, Q_BLOCK, HEAD_DIM), lambda b, n: (0, row(b, n), 0)),
            pl.BlockSpec((IDX_HEADS, Q_BLOCK, IDX_DIM), lambda b, n: (1, row(b, n), 0)),
            pl.BlockSpec((Q_BLOCK, LANE), lambda b, n: (row(b, n), W_IDX_COL // LANE)),
            _resident((None, s, IDX_DIM), lambda b, n: (b, 0, 0)),
            _resident((None, s, KV_RANK), lambda b, n: (b, 0, 0)),
            _resident((None, KV_RANK, s), lambda b, n: (b, 0, 0)),
            _resident((B_HEADS, HEAD_DIM, KV_RANK), lambda b, n: (0, 0, 0)),
            _resident((B_HEADS, KV_RANK, HEAD_DIM), lambda b, n: (0, 0, 0)),
            _resident((2, Q_BLOCK, hq), lambda b, n: (0, 0, 0)),
            _resident((P2_CHUNK, P2_CHUNK), lambda b, n: (0, 0)),
        ],
        out_specs=pl.BlockSpec((Q_BLOCK, B_WIDTH), lambda b, n: (row(b, n), 0)),
        scratch_shapes=[
            pltpu.VMEM((s, Q_BLOCK), jnp.int32),
            pltpu.VMEM((hq, KV_RANK), BF16),
            pltpu.VMEM((KV_RANK, hq), F32),
            pltpu.VMEM((1, hq), F32),
            pltpu.VMEM((1, hq), F32),
            pltpu.VMEM((1, hq), F32),
            pltpu.VMEM((P2_CHUNK, hq), BF16),
        ],
        compiler_params=_params(("arbitrary", "arbitrary"), 56),
        name="dsa_attention",
    )(qbi, qbi, latent, kidx, ckv, ckvt, wuk_t, wuv_h, tab, tri)


def _merge_kernel(u_ref, oa_ref, ob_ref, wga_ref, wgb_ref, wpa_ref, wpb_ref, y_ref):
    u = u_ref[...]
    ga = jnp.dot(u, wga_ref[...], preferred_element_type=F32)
    gb = jnp.dot(u, wgb_ref[...], preferred_element_type=F32)
    pa = jnp.dot(oa_ref[...], wpa_ref[...], preferred_element_type=F32)
    pb = jnp.dot(ob_ref[...], wpb_ref[...], preferred_element_type=F32)
    y_ref[...] = (jax.nn.sigmoid(ga) * pa + jax.nn.sigmoid(gb) * pb).astype(y_ref.dtype)


def gated_merge(u, o_a, o_b, w_ga, w_gb, w_pa, w_pb, tm=512, tn=512):
    m, d = u.shape
    row = lambda i, j: (i, 0)
    col = lambda i, j: (0, j)
    return pl.pallas_call(
        _merge_kernel,
        out_shape=jax.ShapeDtypeStruct((m, d), BF16),
        grid=(m // tm, d // tn),
        in_specs=[pl.BlockSpec((tm, d), row), pl.BlockSpec((tm, A_WIDTH), row), pl.BlockSpec((tm, B_WIDTH), row),
                  pl.BlockSpec((d, tn), col), pl.BlockSpec((d, tn), col),
                  pl.BlockSpec((A_WIDTH, tn), col), pl.BlockSpec((B_WIDTH, tn), col)],
        out_specs=pl.BlockSpec((tm, tn), lambda i, j: (i, j)),
        compiler_params=_params(("parallel", "arbitrary"), 56),
        name="gated_merge",
    )(u, o_a, o_b, w_ga, w_gb, w_pa, w_pb)


def _xattn_kernel(x_ref, gx_ref, wq_ref, kv_ref, wo_ref, gf_ref, wr_ref, x2_ref, u3_ref, rl_ref):
    x = x_ref[...]
    u = _rms(x, gx_ref[...]).astype(BF16)
    q = jnp.dot(u, wq_ref[...], preferred_element_type=F32).astype(BF16)
    outs = []
    for h in range(MEM_HEADS):
        hs = slice(h * HEAD_DIM, (h + 1) * HEAD_DIM)
        lg = lax.dot_general(q[:, hs], kv_ref[:, hs], NT_DIMS, preferred_element_type=F32) * (HEAD_DIM ** -0.5)
        p = jnp.exp(lg - jnp.max(lg, axis=-1, keepdims=True))
        p = p * (1.0 / jnp.sum(p, axis=-1, keepdims=True))
        v = kv_ref[:, MEM_WIDTH + h * HEAD_DIM:MEM_WIDTH + (h + 1) * HEAD_DIM]
        outs.append(jnp.dot(p.astype(BF16), v, preferred_element_type=F32).astype(BF16))
    o = jnp.concatenate(outs, axis=1)
    x2 = x + jnp.dot(o, wo_ref[...], preferred_element_type=F32)
    x2_ref[...] = x2
    u3 = _rms(x2, gf_ref[...]).astype(BF16)
    u3_ref[...] = u3
    rl_ref[...] = jnp.dot(u3, wr_ref[...], preferred_element_type=F32)


def cross_attention(x, g_x, w_qm, kv, w_om, g_ffn, w_router, bsz, s, tm=256):
    m, d = x.shape
    per = s // tm
    n_mem = kv.shape[1]
    const = lambda i: (0, 0)
    return pl.pallas_call(
        _xattn_kernel,
        out_shape=(jax.ShapeDtypeStruct((m, d), F32), jax.ShapeDtypeStruct((m, d), BF16),
                   jax.ShapeDtypeStruct((m, ROUTER_WIDTH), F32)),
        grid=(m // tm,),
        in_specs=[pl.BlockSpec((tm, d), lambda i: (i, 0)), pl.BlockSpec((1, d), const),
                  _resident((d, MEM_WIDTH), const),
                  pl.BlockSpec((None, n_mem, 2 * MEM_WIDTH), lambda i: (i // per, 0, 0)),
                  _resident((MEM_WIDTH, d), const), pl.BlockSpec((1, d), const),
                  _resident((d, ROUTER_WIDTH), const)],
        out_specs=(pl.BlockSpec((tm, d), lambda i: (i, 0)), pl.BlockSpec((tm, d), lambda i: (i, 0)),
                   pl.BlockSpec((tm, ROUTER_WIDTH), lambda i: (i, 0))),
        compiler_params=_params(("parallel",), 56),
        name="cross_attention",
    )(x, g_x.reshape(1, d), w_qm, kv, w_om, g_ffn.reshape(1, d), w_router)


def _moe_kernel(be_ref, nu_ref, h_ref, w1_ref, w3_ref, w2_ref, y_ref):
    i = pl.program_id(0)
    c = pl.program_id(1)
    used = i < nu_ref[0]

    @pl.when(used)
    def _():
        h = h_ref[...]
        a = jnp.dot(h, w1_ref[...].astype(BF16), preferred_element_type=F32)
        b = jnp.dot(h, w3_ref[...].astype(BF16), preferred_element_type=F32)
        g = (a * jax.nn.sigmoid(a) * b).astype(BF16)
        half = y_ref.shape[1] // 2
        for dc in range(2):
            cols = slice(dc * half, (dc + 1) * half)
            y = jnp.dot(g, w2_ref[:, cols].astype(BF16), preferred_element_type=F32)

            @pl.when(c == 0)
            def _():
                y_ref[:, cols] = y

            @pl.when(c > 0)
            def _():
                y_ref[:, cols] += y

    @pl.when(jnp.logical_not(used) & (c == 0))
    def _():
        y_ref[...] = jnp.zeros(y_ref.shape, y_ref.dtype)


def moe_ffn(hs, blk_e, n_used, w1, w3, w2):
    n_slots, d = hs.shape
    n_blk = n_slots // MOE_ROWS
    n_c = EXPERT_FF // FF_CHUNK

    def blk(i, nu):
        return jnp.minimum(i, nu[0] - 1)

    def up_map(i, c, be, nu):
        return (be[blk(i, nu)], 0, jnp.where(i < nu[0], c, n_c - 1))

    def down_map(i, c, be, nu):
        return (be[blk(i, nu)], jnp.where(i < nu[0], c, n_c - 1), 0)

    grid_spec = pltpu.PrefetchScalarGridSpec(
        num_scalar_prefetch=2,
        grid=(n_blk, n_c),
        in_specs=[_resident((MOE_ROWS, d), lambda i, c, be, nu: (i, 0)),
                  pl.BlockSpec((None, d, FF_CHUNK), up_map), pl.BlockSpec((None, d, FF_CHUNK), up_map),
                  pl.BlockSpec((None, FF_CHUNK, d), down_map)],
        out_specs=pl.BlockSpec((MOE_ROWS, d), lambda i, c, be, nu: (i, 0)),
    )
    return pl.pallas_call(
        _moe_kernel,
        out_shape=jax.ShapeDtypeStruct((n_slots, d), F32),
        grid_spec=grid_spec,
        compiler_params=_params(("arbitrary", "arbitrary"), 58),
        name="moe_ffn",
    )(blk_e, n_used, hs, w1, w3, w2)


def _combine_kernel(x_ref, y1_ref, y2_ref, w_ref, g_ref, o_ref):
    w = w_ref[...]
    x3 = x_ref[...] + (y1_ref[...] * w[:, 0:1] + y2_ref[...] * w[:, 1:2])
    o_ref[...] = _rms(x3, g_ref[...])


def combine_norm(x, y1, y2, w, g, tm=256):
    m, d = x.shape
    row = lambda i: (i, 0)
    return pl.pallas_call(
        _combine_kernel,
        out_shape=jax.ShapeDtypeStruct((m, d), F32),
        grid=(m // tm,),
        in_specs=[pl.BlockSpec((tm, d), row), pl.BlockSpec((tm, d), row), pl.BlockSpec((tm, d), row),
                  pl.BlockSpec((tm, EXPERT_TOPK), row), pl.BlockSpec((1, d), lambda i: (0, 0))],
        out_specs=pl.BlockSpec((tm, d), row),
        compiler_params=_params(("parallel",), 48),
        name="combine_norm",
    )(x, y1, y2, w, g.reshape(1, d))


def _rel_bucket(dist):
    max_exact = N_BUCKETS // 2
    d = jnp.maximum(dist, 0)
    df = jnp.maximum(d, 1).astype(F32)
    large = max_exact + (jnp.log(df / max_exact) / math.log(MAX_DISTANCE / max_exact)
                         * (N_BUCKETS - max_exact)).astype(jnp.int32)
    large = jnp.minimum(large, N_BUCKETS - 1)
    return jnp.where(d < max_exact, d, large)


def _swa_bias(bias_table):
    qi = jnp.arange(WINDOW, dtype=jnp.int32)[:, None]
    sj = jnp.arange(2 * WINDOW, dtype=jnp.int32)[None, :]
    dist = qi + WINDOW - sj
    bias = jnp.transpose(bias_table[_rel_bucket(dist)], (2, 0, 1)).astype(F32)
    bias = jnp.where(((dist >= 0) & (dist < WINDOW))[None], bias, -jnp.inf)
    return bias.reshape(A_KV_HEADS, A_GROUP * WINDOW, 2 * WINDOW)


def _dsa_bias(bias_table):
    kk = jnp.arange(Q_BLOCK, dtype=jnp.int32)[:, None]
    qq = jnp.arange(Q_BLOCK, dtype=jnp.int32)[None, :]
    tabs = []
    for dist in (Q_BLOCK + qq - kk, qq - kk):
        b = bias_table[_rel_bucket(dist)].astype(F32) - bias_table[N_BUCKETS - 1].astype(F32)
        tabs.append(jnp.transpose(b, (0, 2, 1)).reshape(Q_BLOCK, B_HEADS * Q_BLOCK))
    return jnp.stack(tabs)


def _moe_routing(rl, b_grp, b_exp):
    n = rl.shape[0]
    grp_logits = rl[:, :N_GROUPS] + b_grp.astype(F32)
    grp_p = jax.nn.softmax(grp_logits, axis=-1)
    g_top = jnp.argmax(grp_logits, axis=-1).astype(jnp.int32)
    g_gate = jnp.take_along_axis(grp_p, g_top[:, None], axis=1)
    exp_logits = (rl[:, N_GROUPS:N_GROUPS + N_EXPERTS] + b_exp.astype(F32)).reshape(n, N_GROUPS, EXPERTS_PER_GROUP)
    in_grp = jnp.take_along_axis(exp_logits, g_top[:, None, None], axis=1)[:, 0]
    top_v, top_j = lax.top_k(in_grp, EXPERT_TOPK)
    gate = g_gate * jax.nn.softmax(top_v, axis=-1)
    eid = (g_top[:, None] * EXPERTS_PER_GROUP + top_j).reshape(-1).astype(jnp.int32)
    n_assign = n * EXPERT_TOPK
    tok_f = jnp.repeat(jnp.arange(n, dtype=jnp.int32), EXPERT_TOPK)
    order = jnp.argsort(eid)
    se = eid[order]
    counts = jnp.bincount(eid, length=N_EXPERTS).astype(jnp.int32)
    padded = ((counts + MOE_ROWS - 1) // MOE_ROWS) * MOE_ROWS
    pad_end = jnp.cumsum(padded)
    pad_start = pad_end - padded
    start = jnp.cumsum(counts) - counts
    dest = pad_start[se] + (jnp.arange(n_assign, dtype=jnp.int32) - start[se])
    n_slots = n_assign + N_EXPERTS * MOE_ROWS
    n_blk = n_slots // MOE_ROWS
    slot_tok = jnp.full((n_slots,), n, jnp.int32).at[dest].set(tok_f[order])
    pos = jnp.zeros((n_assign,), jnp.int32).at[order].set(dest).reshape(n, EXPERT_TOPK)
    blk_e = jnp.minimum(jnp.searchsorted(pad_end, jnp.arange(n_blk, dtype=jnp.int32) * MOE_ROWS, side='right'),
                        N_EXPERTS - 1).astype(jnp.int32)
    n_used = (pad_end[-1:] // MOE_ROWS).astype(jnp.int32)
    return gate, slot_tok, pos, blk_e, n_used


def _layer(x, mem, rel_bias, g_mix, w_in, g_cq, w_uq, w_qidx, g_ckv, w_uk, w_uv, g_kidx, sink_a, w_pa, w_pb, w_out,
           g_xattn, g_mem, w_qm, w_km, w_vm, w_om, g_ffn, w_grp, b_grp, w_exp, b_exp, w_e1, w_e3, w_e2):
    bsz, s, d = x.shape
    n = bsz * s
    xt = x.reshape(n, d)
    c_qkv = A_WIDTH + 2 * KV_WIDTH
    c_lat = c_qkv + Q_RANK + KV_RANK + IDX_DIM + IDX_HEADS
    w_qkv = w_in[:, :c_qkv].astype(BF16)
    w_lat = jnp.pad(w_in[:, c_qkv:c_lat], ((0, 0), (0, LATENT_WIDTH - (c_lat - c_qkv)))).astype(BF16)
    w_ga = w_in[:, c_lat:c_lat + d].astype(BF16)
    w_gb = w_in[:, c_lat + d:].astype(BF16)

    u = rmsnorm_cast(xt, g_mix)
    qkv = matmul(u, w_qkv, BF16)
    latent = matmul(u, w_lat, F32)
    o_a = swa_attention(qkv, sink_a.astype(F32), _swa_bias(rel_bias[:, :A_HEADS]), bsz, s)

    ckv, ckvt, kidx = latent_norm(latent, g_ckv, g_kidx, bsz, s)
    qbi = rms_matmul_heads(latent, g_cq, jnp.concatenate([w_uq, w_qidx], axis=1).astype(BF16))
    tri = jnp.tril(jnp.ones((P2_CHUNK, P2_CHUNK), BF16))
    o_b = dsa_attention(qbi, latent, kidx, ckv, ckvt, jnp.transpose(w_uk, (1, 2, 0)).astype(BF16),
                        jnp.transpose(w_uv, (1, 0, 2)).astype(BF16), _dsa_bias(rel_bias[:, A_HEADS:]), tri, bsz, s)

    y = gated_merge(u, o_a, o_b, w_ga, w_gb, w_pa.astype(BF16), w_pb.astype(BF16))
    x1 = matmul_residual(y, w_out.astype(BF16), xt)

    n_mem = mem.shape[1]
    um = rmsnorm_cast(mem.reshape(bsz * n_mem, d), g_mem)
    kv = matmul(um, jnp.concatenate([w_km, w_vm], axis=1).astype(BF16), BF16, tm=bsz * n_mem)
    w_router = jnp.pad(jnp.concatenate([w_grp, w_exp], axis=1),
                       ((0, 0), (0, ROUTER_WIDTH - N_GROUPS - N_EXPERTS))).astype(BF16)
    x2, u3, rl = cross_attention(x1, g_xattn, w_qm.astype(BF16), kv.reshape(bsz, n_mem, 2 * MEM_WIDTH),
                                 w_om.astype(BF16), g_ffn, w_router, bsz, s)

    gate, slot_tok, pos, blk_e, n_used = _moe_routing(rl, b_grp, b_exp)
    hs = jnp.take(u3, slot_tok, axis=0, mode='fill', fill_value=0)
    ys = moe_ffn(hs, blk_e, n_used, w_e1, w_e3, w_e2)
    y1 = jnp.take(ys, pos[:, 0], axis=0)
    y2 = jnp.take(ys, pos[:, 1], axis=0)
    return x2, y1, y2, gate


def kernel(x, mem, rel_bias, g_mix, w_in, g_cq, w_uq, w_qidx, g_ckv, w_uk, w_uv, g_kidx, sink_a, w_pa, w_pb, w_out,
           g_xattn, g_mem, w_qm, w_km, w_vm, w_om, g_ffn, w_grp, b_grp, w_exp, b_exp, w_e1, w_e3, w_e2, g_final):
    assert g_mix.shape[0] == 1, "one layer"
    bsz, s, d = x.shape
    x2, y1, y2, gate = _layer(x, mem, rel_bias, g_mix[0], w_in[0], g_cq[0], w_uq[0], w_qidx[0], g_ckv[0], w_uk[0],
                              w_uv[0], g_kidx[0], sink_a[0], w_pa[0], w_pb[0], w_out[0], g_xattn[0], g_mem[0],
                              w_qm[0], w_km[0], w_vm[0], w_om[0], g_ffn[0], w_grp[0], b_grp[0], w_exp[0], b_exp[0],
                              w_e1[0], w_e3[0], w_e2[0])
    out = combine_norm(x2, y1, y2, gate, g_final)
    return out.reshape(bsz, s, d)
```

```python
import functools
import math

import numpy as np
import jax
import jax.numpy as jnp
from jax import lax
from jax.experimental import pallas as pl
from jax.experimental.pallas import tpu as pltpu

F32 = jnp.float32
BF16 = jnp.bfloat16

HEAD_DIM = 128
A_HEADS = 16
A_KV_HEADS = 4
A_GROUP = A_HEADS // A_KV_HEADS
WINDOW = 128
A_WIDTH = A_HEADS * HEAD_DIM
KV_WIDTH = A_KV_HEADS * HEAD_DIM
B_HEADS = 16
B_WIDTH = B_HEADS * HEAD_DIM
Q_RANK = 1024
KV_RANK = 512
IDX_HEADS = 16
IDX_DIM = 128
TOPK_KEYS = 256
Q_BLOCK = 128
N_BUCKETS = 32
MAX_DISTANCE = 128
MEM_HEADS = 4
MEM_WIDTH = MEM_HEADS * HEAD_DIM
N_GROUPS = 8
EXPERTS_PER_GROUP = 8
N_EXPERTS = N_GROUPS * EXPERTS_PER_GROUP
EXPERT_FF = 768
EXPERT_TOPK = 2
EPS = 1e-6

LANE = 128
LATENT_WIDTH = 2048
W_IDX_COL = Q_RANK + KV_RANK + IDX_DIM
MOE_ROWS = 512
FF_CHUNK = 256
ROUTER_WIDTH = 128
INT_MIN = -(2 ** 31)
KEY_NEG_INF = int(np.array([0xFF800000 ^ 0x7FFFFFFF], np.uint32).view(np.int32)[0])
NEG_BIG = -1e30
NT_DIMS = (((1,), (1,)), ((), ()))


def _params(semantics, vmem_mb):
    return pltpu.CompilerParams(dimension_semantics=semantics, vmem_limit_bytes=int(vmem_mb * 2 ** 20))


def _rms(xf, g):
    return xf * lax.rsqrt(jnp.mean(xf * xf, axis=-1, keepdims=True) + EPS) * g


def _resident(block, index_map):
    return pl.BlockSpec(block, index_map, pipeline_mode=pl.Buffered(1))


def _rmsnorm_kernel(x_ref, g_ref, o_ref):
    o_ref[...] = _rms(x_ref[...], g_ref[...]).astype(o_ref.dtype)


def rmsnorm_cast(x, g, tm=512):
    m, k = x.shape
    return pl.pallas_call(
        _rmsnorm_kernel,
        out_shape=jax.ShapeDtypeStruct((m, k), BF16),
        grid=(m // tm,),
        in_specs=[pl.BlockSpec((tm, k), lambda i: (i, 0)), pl.BlockSpec((1, k), lambda i: (0, 0))],
        out_specs=pl.BlockSpec((tm, k), lambda i: (i, 0)),
        compiler_params=_params(("parallel",), 40),
        name="rmsnorm_cast",
    )(x, g.reshape(1, k))


def _mm_kernel(a_ref, w_ref, o_ref):
    o_ref[...] = jnp.dot(a_ref[...], w_ref[...], preferred_element_type=F32).astype(o_ref.dtype)


def matmul(a, w, out_dtype, tm=1024, tn=512):
    m, k = a.shape
    n = w.shape[1]
    return pl.pallas_call(
        _mm_kernel,
        out_shape=jax.ShapeDtypeStruct((m, n), out_dtype),
        grid=(m // tm, n // tn),
        in_specs=[pl.BlockSpec((tm, k), lambda i, j: (i, 0)), pl.BlockSpec((k, tn), lambda i, j: (0, j))],
        out_specs=pl.BlockSpec((tm, tn), lambda i, j: (i, j)),
        compiler_params=_params(("parallel", "arbitrary"), 48),
        name="matmul",
    )(a, w)


def _mm_res_kernel(a_ref, w_ref, r_ref, o_ref):
    o_ref[...] = r_ref[...] + jnp.dot(a_ref[...], w_ref[...], preferred_element_type=F32)


def matmul_residual(a, w, res, tm=1024, tn=512):
    m, k = a.shape
    n = w.shape[1]
    return pl.pallas_call(
        _mm_res_kernel,
        out_shape=jax.ShapeDtypeStruct((m, n), F32),
        grid=(m // tm, n // tn),
        in_specs=[pl.BlockSpec((tm, k), lambda i, j: (i, 0)), pl.BlockSpec((k, tn), lambda i, j: (0, j)),
                  pl.BlockSpec((tm, tn), lambda i, j: (i, j))],
        out_specs=pl.BlockSpec((tm, tn), lambda i, j: (i, j)),
        compiler_params=_params(("parallel", "arbitrary"), 48),
        name="matmul_residual",
    )(a, w, res)


def _rms_mm_heads_kernel(x_ref, g_ref, w_ref, o_ref, u_ref):
    @pl.when(pl.program_id(1) == 0)
    def _():
        u_ref[...] = _rms(x_ref[...], g_ref[...]).astype(BF16)

    r = jnp.dot(u_ref[...], w_ref[...], preferred_element_type=F32)
    for hh in range(o_ref.shape[0]):
        o_ref[hh] = r[:, hh * LANE:(hh + 1) * LANE].astype(o_ref.dtype)


def rms_matmul_heads(x, g, w, tm=1024, tn=512):
    m = x.shape[0]
    k, n = w.shape
    hb = tn // LANE
    return pl.pallas_call(
        _rms_mm_heads_kernel,
        out_shape=jax.ShapeDtypeStruct((n // LANE, m, LANE), BF16),
        grid=(m // tm, n // tn),
        in_specs=[pl.BlockSpec((tm, k), lambda i, j: (i, 0)), pl.BlockSpec((1, k), lambda i, j: (0, 0)),
                  pl.BlockSpec((k, tn), lambda i, j: (0, j))],
        out_specs=pl.BlockSpec((hb, tm, LANE), lambda i, j: (j, i, 0)),
        scratch_shapes=[pltpu.VMEM((tm, k), BF16)],
        compiler_params=_params(("parallel", "arbitrary"), 40),
        name="rms_matmul_heads",
    )(x, g.reshape(1, k), w)


def _swa_kernel(sink_ref, q_ref, kp_ref, kc_ref, vp_ref, vc_ref, bias_ref, o_ref):
    n = pl.program_id(1)
    kb = jnp.concatenate([kp_ref[...], kc_ref[...]], axis=0)
    vb = jnp.concatenate([vp_ref[...], vc_ref[...]], axis=0)
    col = lax.broadcasted_iota(jnp.int32, (1, 2 * WINDOW), 1)
    no_prev = jnp.where((col < WINDOW) & (n == 0), -jnp.inf, 0.0)
    for kh in range(A_KV_HEADS):
        heads = [kh * A_GROUP + g for g in range(A_GROUP)]
        qs = jnp.concatenate([q_ref[:, h * HEAD_DIM:(h + 1) * HEAD_DIM] for h in heads], axis=0)
        k = kb[:, kh * HEAD_DIM:(kh + 1) * HEAD_DIM]
        v = vb[:, kh * HEAD_DIM:(kh + 1) * HEAD_DIM]
        lg = lax.dot_general(qs, k, NT_DIMS, preferred_element_type=F32) * (HEAD_DIM ** -0.5)
        lg = lg + bias_ref[kh] + no_prev
        sk = jnp.concatenate([jnp.full((WINDOW, 1), sink_ref[h], F32) for h in heads], axis=0)
        mx = jnp.maximum(jnp.max(lg, axis=-1, keepdims=True), sk)
        p = jnp.exp(lg - mx)
        den = jnp.sum(p, axis=-1, keepdims=True) + jnp.exp(sk - mx)
        o = jnp.dot((p * (1.0 / den)).astype(BF16), v, preferred_element_type=F32)
        for g, h in enumerate(heads):
            o_ref[:, h * HEAD_DIM:(h + 1) * HEAD_DIM] = o[g * WINDOW:(g + 1) * WINDOW].astype(o_ref.dtype)


def swa_attention(qkv, sink, bias, bsz, s):
    nb = s // WINDOW
    kcol = A_WIDTH // KV_WIDTH
    cur = lambda b, n: b * nb + n
    prev = lambda b, n: b * nb + jnp.maximum(n - 1, 0)
    return pl.pallas_call(
        _swa_kernel,
        out_shape=jax.ShapeDtypeStruct((bsz * s, A_WIDTH), BF16),
        grid=(bsz, nb),
        in_specs=[
            pl.BlockSpec(memory_space=pltpu.SMEM),
            pl.BlockSpec((WINDOW, A_WIDTH), lambda b, n: (cur(b, n), 0)),
            pl.BlockSpec((WINDOW, KV_WIDTH), lambda b, n: (prev(b, n), kcol)),
            pl.BlockSpec((WINDOW, KV_WIDTH), lambda b, n: (cur(b, n), kcol)),
            pl.BlockSpec((WINDOW, KV_WIDTH), lambda b, n: (prev(b, n), kcol + 1)),
            pl.BlockSpec((WINDOW, KV_WIDTH), lambda b, n: (cur(b, n), kcol + 1)),
            pl.BlockSpec((A_KV_HEADS, A_GROUP * WINDOW, 2 * WINDOW), lambda b, n: (0, 0, 0)),
        ],
        out_specs=pl.BlockSpec((WINDOW, A_WIDTH), lambda b, n: (cur(b, n), 0)),
        compiler_params=_params(("parallel", "arbitrary"), 32),
        name="swa_attention",
    )(sink, qkv, qkv, qkv, qkv, qkv, bias)


def _latent_norm_kernel(x_ref, gkv_ref, gk_ref, ckv_ref, ckvt_ref, kidx_ref):
    c = _rms(x_ref[:, :KV_RANK], gkv_ref[...])
    ckv_ref[...] = c.astype(BF16)
    ckvt_ref[...] = c.T.astype(BF16)
    kidx_ref[...] = _rms(x_ref[:, KV_RANK:KV_RANK + IDX_DIM], gk_ref[...]).astype(BF16)


def latent_norm(latent, g_ckv, g_kidx, bsz, s, tm=512):
    per = s // tm
    half = LATENT_WIDTH // 2
    return pl.pallas_call(
        _latent_norm_kernel,
        out_shape=(jax.ShapeDtypeStruct((bsz, s, KV_RANK), BF16), jax.ShapeDtypeStruct((bsz, KV_RANK, s), BF16),
                   jax.ShapeDtypeStruct((bsz, s, IDX_DIM), BF16)),
        grid=(bsz * per,),
        in_specs=[pl.BlockSpec((tm, half), lambda i: (i, 1)), pl.BlockSpec((1, KV_RANK), lambda i: (0, 0)),
                  pl.BlockSpec((1, IDX_DIM), lambda i: (0, 0))],
        out_specs=(pl.BlockSpec((None, tm, KV_RANK), lambda i: (i // per, i % per, 0)),
                   pl.BlockSpec((None, KV_RANK, tm), lambda i: (i // per, 0, i % per)),
                   pl.BlockSpec((None, tm, IDX_DIM), lambda i: (i // per, i % per, 0))),
        compiler_params=_params(("parallel",), 32),
        name="latent_norm",
    )(latent, g_ckv.reshape(1, KV_RANK), g_kidx.reshape(1, IDX_DIM))


P1_CHUNK = 1024
P2_CHUNK = 256


def _sortable_key(x):
    bits = lax.bitcast_convert_type(x, jnp.int32)
    return bits ^ ((bits >> 31) & 0x7FFFFFFF)


def _dsa_kernel(qb_ref, qi_ref, w_ref, kidx_ref, ckv_ref, ckvt_ref, wuk_ref, wuv_ref, tab_ref, tri_ref, o_ref,
                key_ref, qlat_ref, acc_ref, m_ref, l_ref, alpha_ref, p_ref):
    n = pl.program_id(1)
    nheads = B_HEADS
    lane_q = lax.broadcasted_iota(jnp.int32, (1, Q_BLOCK), 1)
    t_row = n * Q_BLOCK + lane_q

    for h in range(nheads):
        ql = jnp.dot(qb_ref[h], wuk_ref[h], preferred_element_type=F32) * (HEAD_DIM ** -0.5)
        qlat_ref[h * Q_BLOCK:(h + 1) * Q_BLOCK, :] = ql.astype(BF16)

    w_t = w_ref[...].T

    n_p1 = (n * Q_BLOCK + Q_BLOCK + P1_CHUNK - 1) // P1_CHUNK

    def p1_body(c, carry):
        start = pl.multiple_of(c * P1_CHUNK, P1_CHUNK)
        kc = kidx_ref[pl.ds(start, P1_CHUNK), :]
        score = jnp.zeros((P1_CHUNK, Q_BLOCK), F32)
        hg = 4
        for g0 in range(0, IDX_HEADS, hg):
            q4 = qi_ref[g0:g0 + hg].reshape(hg * Q_BLOCK, IDX_DIM)
            st = lax.dot_general(kc, q4, NT_DIMS, preferred_element_type=F32)
            for j in range(hg):
                h = g0 + j
                score = score + jnp.maximum(st[:, j * Q_BLOCK:(j + 1) * Q_BLOCK], 0.0) * w_t[h:h + 1, :]
        score = score * (IDX_DIM ** -0.5 * IDX_HEADS ** -0.5)
        s_idx = start + lax.broadcasted_iota(jnp.int32, (P1_CHUNK, Q_BLOCK), 0)
        score = jnp.where(s_idx <= t_row, score, -jnp.inf)
        key_ref[pl.ds(start, P1_CHUNK), :] = _sortable_key(score)
        return carry

    lax.fori_loop(0, n_p1, p1_body, 0)

    def count(pred_fn):
        def body(c, cnt8):
            sub = 512
            for j in range(P1_CHUNK // sub):
                blk = key_ref[pl.ds(pl.multiple_of(c * P1_CHUNK + j * sub, sub), sub), :]
                hit = jnp.where(pred_fn(blk), 1, 0).astype(jnp.int32)
                cnt8 = cnt8 + hit.reshape(sub // 8, 8, Q_BLOCK).sum(axis=0)
            return cnt8

        cnt8 = lax.fori_loop(0, n_p1, body, jnp.zeros((8, Q_BLOCK), jnp.int32))
        return cnt8.sum(axis=0, keepdims=True)

    def bit_body(i, thr):
        cand = thr + jnp.left_shift(jnp.int32(1), 31 - i)
        return jnp.where(count(lambda blk: blk >= cand) >= TOPK_KEYS, cand, thr)

    thr = lax.fori_loop(0, 32, bit_body, jnp.full((1, Q_BLOCK), INT_MIN, jnp.int32))
    need = (TOPK_KEYS - count(lambda blk: blk > thr)).astype(F32)

    m_ref[...] = jnp.full(m_ref.shape, NEG_BIG, F32)
    l_ref[...] = jnp.zeros(l_ref.shape, F32)
    acc_ref[...] = jnp.zeros(acc_ref.shape, F32)

    def chunk(c, eq_seen, near):
        start = pl.multiple_of(c * P2_CHUNK, P2_CHUNK)
        u = key_ref[pl.ds(start, P2_CHUNK), :]
        eq = u == thr
        pref = jnp.dot(tri_ref[...], jnp.where(eq, 1.0, 0.0).astype(BF16), preferred_element_type=F32)
        sel = ((u > thr) | (eq & (eq_seen + pref <= need))) & (u != KEY_NEG_INF)
        mask = jnp.where(sel, 0.0, -jnp.inf)
        eq_seen = eq_seen + pref[P2_CHUNK - 1:P2_CHUNK, :]
        lg = lax.dot_general(ckv_ref[pl.ds(start, P2_CHUNK), :], qlat_ref[...], NT_DIMS,
                             preferred_element_type=F32)
        if near:
            halves = []
            for half in range(P2_CHUNK // Q_BLOCK):
                blk = c * (P2_CHUNK // Q_BLOCK) + half
                halves.append(jnp.where(blk == n, tab_ref[1], jnp.where(blk == n - 1, tab_ref[0], 0.0)))
        for h in range(nheads):
            hs = slice(h * Q_BLOCK, (h + 1) * Q_BLOCK)
            lh = lg[:, hs] + mask
            if near:
                lh = lh + jnp.concatenate([b[:, hs] for b in halves], axis=0)
            m_old = m_ref[:, hs]
            m_new = jnp.maximum(m_old, jnp.max(lh, axis=0, keepdims=True))
            alpha = jnp.exp(m_old - m_new)
            p = jnp.exp(lh - m_new)
            l_ref[:, hs] = alpha * l_ref[:, hs] + jnp.sum(p, axis=0, keepdims=True)
            m_ref[:, hs] = m_new
            alpha_ref[:, hs] = alpha
            p_ref[:, hs] = p.astype(BF16)
        pv = jnp.dot(ckvt_ref[:, pl.ds(start, P2_CHUNK)], p_ref[...], preferred_element_type=F32)
        acc_ref[...] = acc_ref[...] * alpha_ref[...] + pv
        return eq_seen

    per = P2_CHUNK // Q_BLOCK
    c_end = n // per + 1
    c_near = jnp.maximum(n - 1, 0) // per
    eq_seen = lax.fori_loop(0, c_near, functools.partial(chunk, near=False), jnp.zeros((1, Q_BLOCK), F32))
    lax.fori_loop(c_near, c_end, functools.partial(chunk, near=True), eq_seen)

    for h in range(nheads):
        hs = slice(h * Q_BLOCK, (h + 1) * Q_BLOCK)
        o_lat = (acc_ref[:, hs] * (1.0 / l_ref[:, hs])).T
        o = jnp.dot(o_lat.astype(BF16), wuv_ref[h], preferred_element_type=F32)
        o_ref[:, h * HEAD_DIM:(h + 1) * HEAD_DIM] = o.astype(o_ref.dtype)


def dsa_attention(qbi, latent, kidx, ckv, ckvt, wuk_t, wuv_h, tab, tri, bsz, s):
    nb = s // Q_BLOCK
    row = lambda b, n: b * nb + n
    hq = B_HEADS * Q_BLOCK
    return pl.pallas_call(
        _dsa_kernel,
        out_shape=jax.ShapeDtypeStruct((bsz * s, B_WIDTH), BF16),
        grid=(bsz, nb),
        in_specs=[
            pl.BlockSpec((B_HEADS, Q_BLOCK, HEAD_DIM), lambda b, n: (0, row(b, n), 0)),
            pl.BlockSpec((IDX_HEADS, Q_BLOCK, IDX_DIM), lambda b, n: (1, row(b, n), 0)),
            pl.BlockSpec((Q_BLOCK, LANE), lambda b, n: (row(b, n), W_IDX_COL // LANE)),
            _resident((None, s, IDX_DIM), lambda b, n: (b, 0, 0)),
            _resident((None, s, KV_RANK), lambda b, n: (b, 0, 0)),
            _resident((None, KV_RANK, s), lambda b, n: (b, 0, 0)),
            _resident((B_HEADS, HEAD_DIM, KV_RANK), lambda b, n: (0, 0, 0)),
            _resident((B_HEADS, KV_RANK, HEAD_DIM), lambda b, n: (0, 0, 0)),
            _resident((2, Q_BLOCK, hq), lambda b, n: (0, 0, 0)),
            _resident((P2_CHUNK, P2_CHUNK), lambda b, n: (0, 0)),
        ],
        out_specs=pl.BlockSpec((Q_BLOCK, B_WIDTH), lambda b, n: (row(b, n), 0)),
        scratch_shapes=[
            pltpu.VMEM((s, Q_BLOCK), jnp.int32),
            pltpu.VMEM((hq, KV_RANK), BF16),
            pltpu.VMEM((KV_RANK, hq), F32),
            pltpu.VMEM((1, hq), F32),
            pltpu.VMEM((1, hq), F32),
            pltpu.VMEM((1, hq), F32),
            pltpu.VMEM((P2_CHUNK, hq), BF16),
        ],
        compiler_params=_params(("arbitrary", "arbitrary"), 56),
        name="dsa_attention",
    )(qbi, qbi, latent, kidx, ckv, ckvt, wuk_t, wuv_h, tab, tri)


def _merge_kernel(u_ref, oa_ref, ob_ref, wga_ref, wgb_ref, wpa_ref, wpb_ref, y_ref):
    u = u_ref[...]
    ga = jnp.dot(u, wga_ref[...], preferred_element_type=F32)
    gb = jnp.dot(u, wgb_ref[...], preferred_element_type=F32)
    pa = jnp.dot(oa_ref[...], wpa_ref[...], preferred_element_type=F32)
    pb = jnp.dot(ob_ref[...], wpb_ref[...], preferred_element_type=F32)
    y_ref[...] = (jax.nn.sigmoid(ga) * pa + jax.nn.sigmoid(gb) * pb).astype(y_ref.dtype)


def gated_merge(u, o_a, o_b, w_ga, w_gb, w_pa, w_pb, tm=512, tn=512):
    m, d = u.shape
    row = lambda i, j: (i, 0)
    col = lambda i, j: (0, j)
    return pl.pallas_call(
        _merge_kernel,
        out_shape=jax.ShapeDtypeStruct((m, d), BF16),
        grid=(m // tm, d // tn),
        in_specs=[pl.BlockSpec((tm, d), row), pl.BlockSpec((tm, A_WIDTH), row), pl.BlockSpec((tm, B_WIDTH), row),
                  pl.BlockSpec((d, tn), col), pl.BlockSpec((d, tn), col),
                  pl.BlockSpec((A_WIDTH, tn), col), pl.BlockSpec((B_WIDTH, tn), col)],
        out_specs=pl.BlockSpec((tm, tn), lambda i, j: (i, j)),
        compiler_params=_params(("parallel", "arbitrary"), 56),
        name="gated_merge",
    )(u, o_a, o_b, w_ga, w_gb, w_pa, w_pb)


def _xattn_kernel(x_ref, gx_ref, wq_ref, kv_ref, wo_ref, gf_ref, wr_ref, x2_ref, u3_ref, rl_ref):
    x = x_ref[...]
    u = _rms(x, gx_ref[...]).astype(BF16)
    q = jnp.dot(u, wq_ref[...], preferred_element_type=F32).astype(BF16)
    outs = []
    for h in range(MEM_HEADS):
        hs = slice(h * HEAD_DIM, (h + 1) * HEAD_DIM)
        lg = lax.dot_general(q[:, hs], kv_ref[:, hs], NT_DIMS, preferred_element_type=F32) * (HEAD_DIM ** -0.5)
        p = jnp.exp(lg - jnp.max(lg, axis=-1, keepdims=True))
        p = p * (1.0 / jnp.sum(p, axis=-1, keepdims=True))
        v = kv_ref[:, MEM_WIDTH + h * HEAD_DIM:MEM_WIDTH + (h + 1) * HEAD_DIM]
        outs.append(jnp.dot(p.astype(BF16), v, preferred_element_type=F32).astype(BF16))
    o = jnp.concatenate(outs, axis=1)
    x2 = x + jnp.dot(o, wo_ref[...], preferred_element_type=F32)
    x2_ref[...] = x2
    u3 = _rms(x2, gf_ref[...]).astype(BF16)
    u3_ref[...] = _pack_bf16_pairs(u3)
    rl_ref[...] = jnp.dot(u3, wr_ref[...], preferred_element_type=F32)


def _pack_bf16_pairs(a):
    k = a.shape[1] // 2
    lo = lax.bitcast_convert_type(a[:, :k].astype(F32), jnp.int32)
    hi = lax.bitcast_convert_type(a[:, k:].astype(F32), jnp.int32)
    return ((lo >> 16) & 0xFFFF) | (hi & jnp.int32(-65536))


def _unpack_bf16_pairs(w):
    lo = lax.bitcast_convert_type(w << 16, F32).astype(BF16)
    hi = lax.bitcast_convert_type(w & jnp.int32(-65536), F32).astype(BF16)
    return lo, hi


def cross_attention(x, g_x, w_qm, kv, w_om, g_ffn, w_router, bsz, s, tm=256):
    m, d = x.shape
    per = s // tm
    n_mem = kv.shape[1]
    const = lambda i: (0, 0)
    return pl.pallas_call(
        _xattn_kernel,
        out_shape=(jax.ShapeDtypeStruct((m, d), F32), jax.ShapeDtypeStruct((m, d // 2), jnp.int32),
                   jax.ShapeDtypeStruct((m, ROUTER_WIDTH), F32)),
        grid=(m // tm,),
        in_specs=[pl.BlockSpec((tm, d), lambda i: (i, 0)), pl.BlockSpec((1, d), const),
                  _resident((d, MEM_WIDTH), const),
                  pl.BlockSpec((None, n_mem, 2 * MEM_WIDTH), lambda i: (i // per, 0, 0)),
                  _resident((MEM_WIDTH, d), const), pl.BlockSpec((1, d), const),
                  _resident((d, ROUTER_WIDTH), const)],
        out_specs=(pl.BlockSpec((tm, d), lambda i: (i, 0)), pl.BlockSpec((tm, d // 2), lambda i: (i, 0)),
                   pl.BlockSpec((tm, ROUTER_WIDTH), lambda i: (i, 0))),
        compiler_params=_params(("parallel",), 56),
        name="cross_attention",
    )(x, g_x.reshape(1, d), w_qm, kv, w_om, g_ffn.reshape(1, d), w_router)


def _moe_kernel(be_ref, nu_ref, nr_ref, tok_ref, tok_next_ref, dst_ref, u_hbm, w1_ref, w3_ref, w2_ref, y_hbm,
                hbuf, h_scr, y_scr, gsem, ssem):
    i = pl.program_id(0)
    c = pl.program_id(1)
    n_c = pl.num_programs(1)
    n_used = nu_ref[0]
    used = i < n_used

    def gather_row(tok, r):
        return pltpu.make_async_copy(u_hbm.at[pl.ds(tok, 1), :], hbuf.at[pl.ds(r, 1), :], gsem)

    def scatter_row(r, dst):
        return pltpu.make_async_copy(y_scr.at[pl.ds(r, 1), :], y_hbm.at[pl.ds(dst, 1), :], ssem)

    def start_gather(t_ref, count):
        def body(r, carry):
            gather_row(t_ref[0, r], r).start()
            return carry
        lax.fori_loop(0, count, body, 0)

    def wait_gather(count):
        def body(r, carry):
            gather_row(0, r).wait()
            return carry
        lax.fori_loop(0, count, body, 0)

    def wait_scatter(count):
        def body(r, carry):
            scatter_row(r, 0).wait()
            return carry
        lax.fori_loop(0, count, body, 0)

    @pl.when(used & (c == 0))
    def _():
        @pl.when(i == 0)
        def _():
            hbuf[...] = jnp.zeros(hbuf.shape, hbuf.dtype)
            start_gather(tok_ref, nr_ref[0])

        wait_gather(nr_ref[i])
        half = h_scr.shape[1] // 2
        piece = 128
        for r0 in range(0, MOE_ROWS, piece):
            lo, hi = _unpack_bf16_pairs(hbuf[r0:r0 + piece, :])
            h_scr[r0:r0 + piece, :half] = lo
            h_scr[r0:r0 + piece, half:] = hi

        @pl.when(i + 1 < n_used)
        def _():
            start_gather(tok_next_ref, nr_ref[jnp.minimum(i + 1, nr_ref.shape[0] - 1)])

    @pl.when(used)
    def _():
        h = h_scr[...]
        a = jnp.dot(h, w1_ref[...].astype(BF16), preferred_element_type=F32)
        b = jnp.dot(h, w3_ref[...].astype(BF16), preferred_element_type=F32)
        g = (a * jax.nn.sigmoid(a) * b).astype(BF16)

        @pl.when((c == 0) & (i > 0))
        def _():
            wait_scatter(nr_ref[jnp.maximum(i - 1, 0)])

        half = y_scr.shape[1] // 2
        for dc in range(2):
            cols = slice(dc * half, (dc + 1) * half)
            y = jnp.dot(g, w2_ref[:, cols].astype(BF16), preferred_element_type=F32)

            @pl.when(c == 0)
            def _():
                y_scr[:, cols] = y

            @pl.when(c > 0)
            def _():
                y_scr[:, cols] += y

        @pl.when(c == n_c - 1)
        def _():
            def body(r, carry):
                scatter_row(r, dst_ref[0, r]).start()
                return carry
            lax.fori_loop(0, nr_ref[i], body, 0)

            @pl.when(i == n_used - 1)
            def _():
                wait_scatter(nr_ref[i])


def moe_ffn(u_packed, slot_tok, slot_dst, blk_e, n_used, n_real, w1, w3, w2):
    n, dh = u_packed.shape
    d = 2 * dh
    n_blk = slot_tok.shape[0]
    n_c = EXPERT_FF // FF_CHUNK

    def blk(i, nu):
        return jnp.minimum(i, nu[0] - 1)

    def up_map(i, c, be, nu, nr):
        return (be[blk(i, nu)], 0, jnp.where(i < nu[0], c, n_c - 1))

    def down_map(i, c, be, nu, nr):
        return (be[blk(i, nu)], jnp.where(i < nu[0], c, n_c - 1), 0)

    def slots(shift):
        return pl.BlockSpec((None, 1, MOE_ROWS), lambda i, c, be, nu, nr: (jnp.minimum(i + shift, n_blk - 1), 0, 0),
                            memory_space=pltpu.SMEM)

    grid_spec = pltpu.PrefetchScalarGridSpec(
        num_scalar_prefetch=3,
        grid=(n_blk, n_c),
        in_specs=[slots(0), slots(1), slots(0), pl.BlockSpec(memory_space=pl.ANY),
                  pl.BlockSpec((None, d, FF_CHUNK), up_map), pl.BlockSpec((None, d, FF_CHUNK), up_map),
                  pl.BlockSpec((None, FF_CHUNK, d), down_map)],
        out_specs=pl.BlockSpec(memory_space=pl.ANY),
        scratch_shapes=[pltpu.VMEM((MOE_ROWS, dh), jnp.int32), pltpu.VMEM((MOE_ROWS, d), BF16),
                        pltpu.VMEM((MOE_ROWS, d), F32), pltpu.SemaphoreType.DMA(()), pltpu.SemaphoreType.DMA(())],
    )
    return pl.pallas_call(
        _moe_kernel,
        out_shape=jax.ShapeDtypeStruct((EXPERT_TOPK * n, d), F32),
        grid_spec=grid_spec,
        compiler_params=_params(("arbitrary", "arbitrary"), 58),
        name="moe_ffn",
    )(blk_e, n_used, n_real, slot_tok, slot_tok, slot_dst, u_packed, w1, w3, w2)


def _combine_kernel(x_ref, y1_ref, y2_ref, w_ref, g_ref, o_ref):
    w = w_ref[...]
    x3 = x_ref[...] + (y1_ref[...] * w[:, 0:1] + y2_ref[...] * w[:, 1:2])
    o_ref[...] = _rms(x3, g_ref[...])


def combine_norm(x, y, w, g, tm=256):
    m, d = x.shape
    row = lambda i: (i, 0)
    return pl.pallas_call(
        _combine_kernel,
        out_shape=jax.ShapeDtypeStruct((m, d), F32),
        grid=(m // tm,),
        in_specs=[pl.BlockSpec((tm, d), row), pl.BlockSpec((None, tm, d), lambda i: (0, i, 0)),
                  pl.BlockSpec((None, tm, d), lambda i: (1, i, 0)),
                  pl.BlockSpec((tm, EXPERT_TOPK), row), pl.BlockSpec((1, d), lambda i: (0, 0))],
        out_specs=pl.BlockSpec((tm, d), row),
        compiler_params=_params(("parallel",), 48),
        name="combine_norm",
    )(x, y, y, w, g.reshape(1, d))


def _rel_bucket(dist):
    max_exact = N_BUCKETS // 2
    d = jnp.maximum(dist, 0)
    df = jnp.maximum(d, 1).astype(F32)
    large = max_exact + (jnp.log(df / max_exact) / math.log(MAX_DISTANCE / max_exact)
                         * (N_BUCKETS - max_exact)).astype(jnp.int32)
    large = jnp.minimum(large, N_BUCKETS - 1)
    return jnp.where(d < max_exact, d, large)


def _swa_bias(bias_table):
    qi = jnp.arange(WINDOW, dtype=jnp.int32)[:, None]
    sj = jnp.arange(2 * WINDOW, dtype=jnp.int32)[None, :]
    dist = qi + WINDOW - sj
    bias = jnp.transpose(bias_table[_rel_bucket(dist)], (2, 0, 1)).astype(F32)
    bias = jnp.where(((dist >= 0) & (dist < WINDOW))[None], bias, -jnp.inf)
    return bias.reshape(A_KV_HEADS, A_GROUP * WINDOW, 2 * WINDOW)


def _dsa_bias(bias_table):
    kk = jnp.arange(Q_BLOCK, dtype=jnp.int32)[:, None]
    qq = jnp.arange(Q_BLOCK, dtype=jnp.int32)[None, :]
    tabs = []
    for dist in (Q_BLOCK + qq - kk, qq - kk):
        b = bias_table[_rel_bucket(dist)].astype(F32) - bias_table[N_BUCKETS - 1].astype(F32)
        tabs.append(jnp.transpose(b, (0, 2, 1)).reshape(Q_BLOCK, B_HEADS * Q_BLOCK))
    return jnp.stack(tabs)


def _moe_routing(rl, b_grp, b_exp):
    n = rl.shape[0]
    grp_logits = rl[:, :N_GROUPS] + b_grp.astype(F32)
    grp_p = jax.nn.softmax(grp_logits, axis=-1)
    g_top = jnp.argmax(grp_logits, axis=-1).astype(jnp.int32)
    g_gate = jnp.take_along_axis(grp_p, g_top[:, None], axis=1)
    exp_logits = (rl[:, N_GROUPS:N_GROUPS + N_EXPERTS] + b_exp.astype(F32)).reshape(n, N_GROUPS, EXPERTS_PER_GROUP)
    in_grp = jnp.take_along_axis(exp_logits, g_top[:, None, None], axis=1)[:, 0]
    top_v, top_j = lax.top_k(in_grp, EXPERT_TOPK)
    gate = g_gate * jax.nn.softmax(top_v, axis=-1)
    eid = (g_top[:, None] * EXPERTS_PER_GROUP + top_j).reshape(-1).astype(jnp.int32)
    n_assign = n * EXPERT_TOPK
    tok_f = jnp.repeat(jnp.arange(n, dtype=jnp.int32), EXPERT_TOPK)
    order = jnp.argsort(eid)
    se = eid[order]
    counts = jnp.bincount(eid, length=N_EXPERTS).astype(jnp.int32)
    padded = ((counts + MOE_ROWS - 1) // MOE_ROWS) * MOE_ROWS
    pad_end = jnp.cumsum(padded)
    pad_start = pad_end - padded
    start = jnp.cumsum(counts) - counts
    dest = pad_start[se] + (jnp.arange(n_assign, dtype=jnp.int32) - start[se])
    n_slots = n_assign + N_EXPERTS * MOE_ROWS
    n_blk = n_slots // MOE_ROWS
    slot_tok = jnp.zeros((n_slots,), jnp.int32).at[dest].set(tok_f[order])
    slot_dst = jnp.zeros((n_slots,), jnp.int32).at[dest].set((order % EXPERT_TOPK) * n + order // EXPERT_TOPK)
    blk_start = jnp.arange(n_blk, dtype=jnp.int32) * MOE_ROWS
    blk_e = jnp.minimum(jnp.searchsorted(pad_end, blk_start, side='right'), N_EXPERTS - 1).astype(jnp.int32)
    n_real = jnp.clip(counts[blk_e] - (blk_start - pad_start[blk_e]), 0, MOE_ROWS).astype(jnp.int32)
    n_used = (pad_end[-1:] // MOE_ROWS).astype(jnp.int32)
    return (gate, slot_tok.reshape(n_blk, 1, MOE_ROWS), slot_dst.reshape(n_blk, 1, MOE_ROWS), blk_e, n_used, n_real)


def _layer(x, mem, rel_bias, g_mix, w_in, g_cq, w_uq, w_qidx, g_ckv, w_uk, w_uv, g_kidx, sink_a, w_pa, w_pb, w_out,
           g_xattn, g_mem, w_qm, w_km, w_vm, w_om, g_ffn, w_grp, b_grp, w_exp, b_exp, w_e1, w_e3, w_e2):
    bsz, s, d = x.shape
    n = bsz * s
    xt = x.reshape(n, d)
    c_qkv = A_WIDTH + 2 * KV_WIDTH
    c_lat = c_qkv + Q_RANK + KV_RANK + IDX_DIM + IDX_HEADS
    w_qkv = w_in[:, :c_qkv].astype(BF16)
    w_lat = jnp.pad(w_in[:, c_qkv:c_lat], ((0, 0), (0, LATENT_WIDTH - (c_lat - c_qkv)))).astype(BF16)
    w_ga = w_in[:, c_lat:c_lat + d].astype(BF16)
    w_gb = w_in[:, c_lat + d:].astype(BF16)

    u = rmsnorm_cast(xt, g_mix)
    qkv = matmul(u, w_qkv, BF16)
    latent = matmul(u, w_lat, F32)
    o_a = swa_attention(qkv, sink_a.astype(F32), _swa_bias(rel_bias[:, :A_HEADS]), bsz, s)

    ckv, ckvt, kidx = latent_norm(latent, g_ckv, g_kidx, bsz, s)
    qbi = rms_matmul_heads(latent, g_cq, jnp.concatenate([w_uq, w_qidx], axis=1).astype(BF16))
    tri = jnp.tril(jnp.ones((P2_CHUNK, P2_CHUNK), BF16))
    o_b = dsa_attention(qbi, latent, kidx, ckv, ckvt, jnp.transpose(w_uk, (1, 2, 0)).astype(BF16),
                        jnp.transpose(w_uv, (1, 0, 2)).astype(BF16), _dsa_bias(rel_bias[:, A_HEADS:]), tri, bsz, s)

    y = gated_merge(u, o_a, o_b, w_ga, w_gb, w_pa.astype(BF16), w_pb.astype(BF16))
    x1 = matmul_residual(y, w_out.astype(BF16), xt)

    n_mem = mem.shape[1]
    um = rmsnorm_cast(mem.reshape(bsz * n_mem, d), g_mem)
    kv = matmul(um, jnp.concatenate([w_km, w_vm], axis=1).astype(BF16), BF16, tm=bsz * n_mem)
    w_router = jnp.pad(jnp.concatenate([w_grp, w_exp], axis=1),
                       ((0, 0), (0, ROUTER_WIDTH - N_GROUPS - N_EXPERTS))).astype(BF16)
    x2, u3, rl = cross_attention(x1, g_xattn, w_qm.astype(BF16), kv.reshape(bsz, n_mem, 2 * MEM_WIDTH),
                                 w_om.astype(BF16), g_ffn, w_router, bsz, s)

    gate, slot_tok, slot_dst, blk_e, n_used, n_real = _moe_routing(rl, b_grp, b_exp)
    ys = moe_ffn(u3, slot_tok, slot_dst, blk_e, n_used, n_real, w_e1, w_e3, w_e2)
    return x2, ys.reshape(EXPERT_TOPK, n, d), gate


def kernel(x, mem, rel_bias, g_mix, w_in, g_cq, w_uq, w_qidx, g_ckv, w_uk, w_uv, g_kidx, sink_a, w_pa, w_pb, w_out,
           g_xattn, g_mem, w_qm, w_km, w_vm, w_om, g_ffn, w_grp, b_grp, w_exp, b_exp, w_e1, w_e3, w_e2, g_final):
    assert g_mix.shape[0] == 1, "one layer"
    bsz, s, d = x.shape
    x2, ys, gate = _layer(x, mem, rel_bias, g_mix[0], w_in[0], g_cq[0], w_uq[0], w_qidx[0], g_ckv[0], w_uk[0],
                              w_uv[0], g_kidx[0], sink_a[0], w_pa[0], w_pb[0], w_out[0], g_xattn[0], g_mem[0],
                              w_qm[0], w_km[0], w_vm[0], w_om[0], g_ffn[0], w_grp[0], b_grp[0], w_exp[0], b_exp[0],
                              w_e1[0], w_e3[0], w_e2[0])
    out = combine_norm(x2, ys, gate, g_final)
    return out.reshape(bsz, s, d)
```

```python
import functools
import math

import numpy as np
import jax
import jax.numpy as jnp
from jax import lax
from jax.experimental import pallas as pl
from jax.experimental.pallas import tpu as pltpu

F32 = jnp.float32
BF16 = jnp.bfloat16

HEAD_DIM = 128
A_HEADS = 16
A_KV_HEADS = 4
A_GROUP = A_HEADS // A_KV_HEADS
WINDOW = 128
A_WIDTH = A_HEADS * HEAD_DIM
KV_WIDTH = A_KV_HEADS * HEAD_DIM
B_HEADS = 16
B_WIDTH = B_HEADS * HEAD_DIM
Q_RANK = 1024
KV_RANK = 512
IDX_HEADS = 16
IDX_DIM = 128
TOPK_KEYS = 256
Q_BLOCK = 128
N_BUCKETS = 32
MAX_DISTANCE = 128
MEM_HEADS = 4
MEM_WIDTH = MEM_HEADS * HEAD_DIM
N_GROUPS = 8
EXPERTS_PER_GROUP = 8
N_EXPERTS = N_GROUPS * EXPERTS_PER_GROUP
EXPERT_FF = 768
EXPERT_TOPK = 2
EPS = 1e-6

LANE = 128
LATENT_WIDTH = 2048
W_IDX_COL = Q_RANK + KV_RANK + IDX_DIM
MOE_ROWS = 512
FF_CHUNK = 256
MOE_SUB = 256
MOE_GATHER_MAIN = 128
MOE_GATHER_REST, _rem = divmod(MOE_ROWS - (EXPERT_FF // FF_CHUNK) * MOE_GATHER_MAIN, EXPERT_FF // FF_CHUNK - 1)
assert _rem == 0 and MOE_ROWS % MOE_SUB == 0
ROUTER_WIDTH = 128
INT_MIN = -(2 ** 31)
KEY_NEG_INF = int(np.array([0xFF800000 ^ 0x7FFFFFFF], np.uint32).view(np.int32)[0])
NEG_BIG = -1e30
NT_DIMS = (((1,), (1,)), ((), ()))


def _params(semantics, vmem_mb):
    return pltpu.CompilerParams(dimension_semantics=semantics, vmem_limit_bytes=int(vmem_mb * 2 ** 20))


def _rms(xf, g):
    return xf * lax.rsqrt(jnp.mean(xf * xf, axis=-1, keepdims=True) + EPS) * g


def _resident(block, index_map):
    return pl.BlockSpec(block, index_map, pipeline_mode=pl.Buffered(1))


def _rmsnorm_kernel(x_ref, g_ref, o_ref):
    o_ref[...] = _rms(x_ref[...], g_ref[...]).astype(o_ref.dtype)


def rmsnorm_cast(x, g, tm=512):
    m, k = x.shape
    return pl.pallas_call(
        _rmsnorm_kernel,
        out_shape=jax.ShapeDtypeStruct((m, k), BF16),
        grid=(m // tm,),
        in_specs=[pl.BlockSpec((tm, k), lambda i: (i, 0)), pl.BlockSpec((1, k), lambda i: (0, 0))],
        out_specs=pl.BlockSpec((tm, k), lambda i: (i, 0)),
        compiler_params=_params(("parallel",), 40),
        name="rmsnorm_cast",
    )(x, g.reshape(1, k))


def _mm_kernel(a_ref, w_ref, o_ref):
    o_ref[...] = jnp.dot(a_ref[...], w_ref[...], preferred_element_type=F32).astype(o_ref.dtype)


def matmul(a, w, out_dtype, tm=1024, tn=512):
    m, k = a.shape
    n = w.shape[1]
    return pl.pallas_call(
        _mm_kernel,
        out_shape=jax.ShapeDtypeStruct((m, n), out_dtype),
        grid=(m // tm, n // tn),
        in_specs=[pl.BlockSpec((tm, k), lambda i, j: (i, 0)), pl.BlockSpec((k, tn), lambda i, j: (0, j))],
        out_specs=pl.BlockSpec((tm, tn), lambda i, j: (i, j)),
        compiler_params=_params(("parallel", "arbitrary"), 48),
        name="matmul",
    )(a, w)


def _mm_res_kernel(a_ref, w_ref, r_ref, o_ref):
    o_ref[...] = r_ref[...] + jnp.dot(a_ref[...], w_ref[...], preferred_element_type=F32)


def matmul_residual(a, w, res, tm=1024, tn=512):
    m, k = a.shape
    n = w.shape[1]
    return pl.pallas_call(
        _mm_res_kernel,
        out_shape=jax.ShapeDtypeStruct((m, n), F32),
        grid=(m // tm, n // tn),
        in_specs=[pl.BlockSpec((tm, k), lambda i, j: (i, 0)), pl.BlockSpec((k, tn), lambda i, j: (0, j)),
                  pl.BlockSpec((tm, tn), lambda i, j: (i, j))],
        out_specs=pl.BlockSpec((tm, tn), lambda i, j: (i, j)),
        compiler_params=_params(("parallel", "arbitrary"), 48),
        name="matmul_residual",
    )(a, w, res)


def _rms_mm_heads_kernel(x_ref, g_ref, w_ref, o_ref, u_ref):
    @pl.when(pl.program_id(1) == 0)
    def _():
        u_ref[...] = _rms(x_ref[...], g_ref[...]).astype(BF16)

    r = jnp.dot(u_ref[...], w_ref[...], preferred_element_type=F32)
    for hh in range(o_ref.shape[0]):
        o_ref[hh] = r[:, hh * LANE:(hh + 1) * LANE].astype(o_ref.dtype)


def rms_matmul_heads(x, g, w, tm=1024, tn=512):
    m = x.shape[0]
    k, n = w.shape
    hb = tn // LANE
    return pl.pallas_call(
        _rms_mm_heads_kernel,
        out_shape=jax.ShapeDtypeStruct((n // LANE, m, LANE), BF16),
        grid=(m // tm, n // tn),
        in_specs=[pl.BlockSpec((tm, k), lambda i, j: (i, 0)), pl.BlockSpec((1, k), lambda i, j: (0, 0)),
                  pl.BlockSpec((k, tn), lambda i, j: (0, j))],
        out_specs=pl.BlockSpec((hb, tm, LANE), lambda i, j: (j, i, 0)),
        scratch_shapes=[pltpu.VMEM((tm, k), BF16)],
        compiler_params=_params(("parallel", "arbitrary"), 40),
        name="rms_matmul_heads",
    )(x, g.reshape(1, k), w)


def _swa_kernel(sink_ref, q_ref, kp_ref, kc_ref, vp_ref, vc_ref, bias_ref, o_ref):
    n = pl.program_id(1)
    kb = jnp.concatenate([kp_ref[...], kc_ref[...]], axis=0)
    vb = jnp.concatenate([vp_ref[...], vc_ref[...]], axis=0)
    col = lax.broadcasted_iota(jnp.int32, (1, 2 * WINDOW), 1)
    no_prev = jnp.where((col < WINDOW) & (n == 0), -jnp.inf, 0.0)
    for kh in range(A_KV_HEADS):
        heads = [kh * A_GROUP + g for g in range(A_GROUP)]
        qs = jnp.concatenate([q_ref[:, h * HEAD_DIM:(h + 1) * HEAD_DIM] for h in heads], axis=0)
        k = kb[:, kh * HEAD_DIM:(kh + 1) * HEAD_DIM]
        v = vb[:, kh * HEAD_DIM:(kh + 1) * HEAD_DIM]
        lg = lax.dot_general(qs, k, NT_DIMS, preferred_element_type=F32) * (HEAD_DIM ** -0.5)
        lg = lg + bias_ref[kh] + no_prev
        sk = jnp.concatenate([jnp.full((WINDOW, 1), sink_ref[h], F32) for h in heads], axis=0)
        mx = jnp.maximum(jnp.max(lg, axis=-1, keepdims=True), sk)
        p = jnp.exp(lg - mx)
        den = jnp.sum(p, axis=-1, keepdims=True) + jnp.exp(sk - mx)
        o = jnp.dot((p * (1.0 / den)).astype(BF16), v, preferred_element_type=F32)
        for g, h in enumerate(heads):
            o_ref[:, h * HEAD_DIM:(h + 1) * HEAD_DIM] = o[g * WINDOW:(g + 1) * WINDOW].astype(o_ref.dtype)


def swa_attention(qkv, sink, bias, bsz, s):
    nb = s // WINDOW
    kcol = A_WIDTH // KV_WIDTH
    cur = lambda b, n: b * nb + n
    prev = lambda b, n: b * nb + jnp.maximum(n - 1, 0)
    return pl.pallas_call(
        _swa_kernel,
        out_shape=jax.ShapeDtypeStruct((bsz * s, A_WIDTH), BF16),
        grid=(bsz, nb),
        in_specs=[
            pl.BlockSpec(memory_space=pltpu.SMEM),
            pl.BlockSpec((WINDOW, A_WIDTH), lambda b, n: (cur(b, n), 0)),
            pl.BlockSpec((WINDOW, KV_WIDTH), lambda b, n: (prev(b, n), kcol)),
            pl.BlockSpec((WINDOW, KV_WIDTH), lambda b, n: (cur(b, n), kcol)),
            pl.BlockSpec((WINDOW, KV_WIDTH), lambda b, n: (prev(b, n), kcol + 1)),
            pl.BlockSpec((WINDOW, KV_WIDTH), lambda b, n: (cur(b, n), kcol + 1)),
            pl.BlockSpec((A_KV_HEADS, A_GROUP * WINDOW, 2 * WINDOW), lambda b, n: (0, 0, 0)),
        ],
        out_specs=pl.BlockSpec((WINDOW, A_WIDTH), lambda b, n: (cur(b, n), 0)),
        compiler_params=_params(("parallel", "arbitrary"), 32),
        name="swa_attention",
    )(sink, qkv, qkv, qkv, qkv, qkv, bias)


def _latent_norm_kernel(x_ref, gkv_ref, gk_ref, ckv_ref, ckvt_ref, kidx_ref):
    c = _rms(x_ref[:, :KV_RANK], gkv_ref[...])
    ckv_ref[...] = c.astype(BF16)
    ckvt_ref[...] = c.T.astype(BF16)
    kidx_ref[...] = _rms(x_ref[:, KV_RANK:KV_RANK + IDX_DIM], gk_ref[...]).astype(BF16)


def latent_norm(latent, g_ckv, g_kidx, bsz, s, tm=512):
    per = s // tm
    half = LATENT_WIDTH // 2
    return pl.pallas_call(
        _latent_norm_kernel,
        out_shape=(jax.ShapeDtypeStruct((bsz, s, KV_RANK), BF16), jax.ShapeDtypeStruct((bsz, KV_RANK, s), BF16),
                   jax.ShapeDtypeStruct((bsz, s, IDX_DIM), BF16)),
        grid=(bsz * per,),
        in_specs=[pl.BlockSpec((tm, half), lambda i: (i, 1)), pl.BlockSpec((1, KV_RANK), lambda i: (0, 0)),
                  pl.BlockSpec((1, IDX_DIM), lambda i: (0, 0))],
        out_specs=(pl.BlockSpec((None, tm, KV_RANK), lambda i: (i // per, i % per, 0)),
                   pl.BlockSpec((None, KV_RANK, tm), lambda i: (i // per, 0, i % per)),
                   pl.BlockSpec((None, tm, IDX_DIM), lambda i: (i // per, i % per, 0))),
        compiler_params=_params(("parallel",), 32),
        name="latent_norm",
    )(latent, g_ckv.reshape(1, KV_RANK), g_kidx.reshape(1, IDX_DIM))


P1_CHUNK = 1024
P2_CHUNK = 512


def _sortable_key(x):
    bits = lax.bitcast_convert_type(x, jnp.int32)
    return bits ^ ((bits >> 31) & 0x7FFFFFFF)


def _dsa_kernel(qb_ref, qi_ref, w_ref, kidx_ref, ckv_ref, ckvt_ref, wuk_ref, wuv_ref, tab_ref, tri_ref, o_ref,
                key_ref, qlat_ref, acc_ref, m_ref, l_ref, alpha_ref, p_ref):
    n = pl.program_id(1)
    nheads = B_HEADS
    lane_q = lax.broadcasted_iota(jnp.int32, (1, Q_BLOCK), 1)
    t_row = n * Q_BLOCK + lane_q

    for h in range(nheads):
        ql = jnp.dot(qb_ref[h], wuk_ref[h], preferred_element_type=F32) * (HEAD_DIM ** -0.5)
        qlat_ref[h * Q_BLOCK:(h + 1) * Q_BLOCK, :] = ql.astype(BF16)

    w_t = w_ref[...].T

    n_p1 = (n * Q_BLOCK + Q_BLOCK + P1_CHUNK - 1) // P1_CHUNK

    def p1_body(c, carry):
        start = pl.multiple_of(c * P1_CHUNK, P1_CHUNK)
        kc = kidx_ref[pl.ds(start, P1_CHUNK), :]
        score = jnp.zeros((P1_CHUNK, Q_BLOCK), F32)
        hg = 4
        for g0 in range(0, IDX_HEADS, hg):
            q4 = qi_ref[g0:g0 + hg].reshape(hg * Q_BLOCK, IDX_DIM)
            st = lax.dot_general(kc, q4, NT_DIMS, preferred_element_type=F32)
            for j in range(hg):
                h = g0 + j
                score = score + jnp.maximum(st[:, j * Q_BLOCK:(j + 1) * Q_BLOCK], 0.0) * w_t[h:h + 1, :]
        score = score * (IDX_DIM ** -0.5 * IDX_HEADS ** -0.5)
        s_idx = start + lax.broadcasted_iota(jnp.int32, (P1_CHUNK, Q_BLOCK), 0)
        score = jnp.where(s_idx <= t_row, score, -jnp.inf)
        key_ref[pl.ds(start, P1_CHUNK), :] = _sortable_key(score)
        return carry

    lax.fori_loop(0, n_p1, p1_body, 0)

    def count(pred_fn):
        def body(c, cnt8):
            sub = 512
            for j in range(P1_CHUNK // sub):
                blk = key_ref[pl.ds(pl.multiple_of(c * P1_CHUNK + j * sub, sub), sub), :]
                hit = jnp.where(pred_fn(blk), 1, 0).astype(jnp.int32)
                cnt8 = cnt8 + hit.reshape(sub // 8, 8, Q_BLOCK).sum(axis=0)
            return cnt8

        cnt8 = lax.fori_loop(0, n_p1, body, jnp.zeros((8, Q_BLOCK), jnp.int32))
        return cnt8.sum(axis=0, keepdims=True)

    def bit_body(i, thr):
        cand = thr + jnp.left_shift(jnp.int32(1), 31 - i)
        return jnp.where(count(lambda blk: blk >= cand) >= TOPK_KEYS, cand, thr)

    thr = lax.fori_loop(0, 32, bit_body, jnp.full((1, Q_BLOCK), INT_MIN, jnp.int32))
    need = (TOPK_KEYS - count(lambda blk: blk > thr)).astype(F32)

    m_ref[...] = jnp.full(m_ref.shape, NEG_BIG, F32)
    l_ref[...] = jnp.zeros(l_ref.shape, F32)
    acc_ref[...] = jnp.zeros(acc_ref.shape, F32)

    def chunk(c, eq_seen, near):
        start = pl.multiple_of(c * P2_CHUNK, P2_CHUNK)
        u = key_ref[pl.ds(start, P2_CHUNK), :]
        eq = u == thr
        pref = jnp.dot(tri_ref[...], jnp.where(eq, 1.0, 0.0).astype(BF16), preferred_element_type=F32)
        sel = ((u > thr) | (eq & (eq_seen + pref <= need))) & (u != KEY_NEG_INF)
        mask = jnp.where(sel, 0.0, -jnp.inf)
        eq_seen = eq_seen + pref[P2_CHUNK - 1:P2_CHUNK, :]
        lg = lax.dot_general(ckv_ref[pl.ds(start, P2_CHUNK), :], qlat_ref[...], NT_DIMS,
                             preferred_element_type=F32)
        if near:
            halves = []
            for half in range(P2_CHUNK // Q_BLOCK):
                blk = c * (P2_CHUNK // Q_BLOCK) + half
                halves.append(jnp.where(blk == n, tab_ref[1], jnp.where(blk == n - 1, tab_ref[0], 0.0)))
        for h in range(nheads):
            hs = slice(h * Q_BLOCK, (h + 1) * Q_BLOCK)
            lh = lg[:, hs] + mask
            if near:
                lh = lh + jnp.concatenate([b[:, hs] for b in halves], axis=0)
            m_old = m_ref[:, hs]
            m_new = jnp.maximum(m_old, jnp.max(lh, axis=0, keepdims=True))
            alpha = jnp.exp(m_old - m_new)
            p = jnp.exp(lh - m_new)
            l_ref[:, hs] = alpha * l_ref[:, hs] + jnp.sum(p, axis=0, keepdims=True)
            m_ref[:, hs] = m_new
            alpha_ref[:, hs] = alpha
            p_ref[:, hs] = p.astype(BF16)
        pv = jnp.dot(ckvt_ref[:, pl.ds(start, P2_CHUNK)], p_ref[...], preferred_element_type=F32)
        acc_ref[...] = acc_ref[...] * alpha_ref[...] + pv
        return eq_seen

    per = P2_CHUNK // Q_BLOCK
    c_end = n // per + 1
    c_near = jnp.maximum(n - 1, 0) // per
    eq_seen = lax.fori_loop(0, c_near, functools.partial(chunk, near=False), jnp.zeros((1, Q_BLOCK), F32))
    lax.fori_loop(c_near, c_end, functools.partial(chunk, near=True), eq_seen)

    for h in range(nheads):
        hs = slice(h * Q_BLOCK, (h + 1) * Q_BLOCK)
        o_lat = (acc_ref[:, hs] * (1.0 / l_ref[:, hs])).T
        o = jnp.dot(o_lat.astype(BF16), wuv_ref[h], preferred_element_type=F32)
        o_ref[:, h * HEAD_DIM:(h + 1) * HEAD_DIM] = o.astype(o_ref.dtype)


def dsa_attention(qbi, latent, kidx, ckv, ckvt, wuk_t, wuv_h, tab, tri, bsz, s):
    nb = s // Q_BLOCK
    row = lambda b, n: b * nb + n
    hq = B_HEADS * Q_BLOCK
    return pl.pallas_call(
        _dsa_kernel,
        out_shape=jax.ShapeDtypeStruct((bsz * s, B_WIDTH), BF16),
        grid=(bsz, nb),
        in_specs=[
            pl.BlockSpec((B_HEADS, Q_BLOCK, HEAD_DIM), lambda b, n: (0, row(b, n), 0)),
            pl.BlockSpec((IDX_HEADS, Q_BLOCK, IDX_DIM), lambda b, n: (1, row(b, n), 0)),
            pl.BlockSpec((Q_BLOCK, LANE), lambda b, n: (row(b, n), W_IDX_COL // LANE)),
            _resident((None, s, IDX_DIM), lambda b, n: (b, 0, 0)),
            _resident((None, s, KV_RANK), lambda b, n: (b, 0, 0)),
            _resident((None, KV_RANK, s), lambda b, n: (b, 0, 0)),
            _resident((B_HEADS, HEAD_DIM, KV_RANK), lambda b, n: (0, 0, 0)),
            _resident((B_HEADS, KV_RANK, HEAD_DIM), lambda b, n: (0, 0, 0)),
            _resident((2, Q_BLOCK, hq), lambda b, n: (0, 0, 0)),
            _resident((P2_CHUNK, P2_CHUNK), lambda b, n: (0, 0)),
        ],
        out_specs=pl.BlockSpec((Q_BLOCK, B_WIDTH), lambda b, n: (row(b, n), 0)),
        scratch_shapes=[
            pltpu.VMEM((s, Q_BLOCK), jnp.int32),
            pltpu.VMEM((hq, KV_RANK), BF16),
            pltpu.VMEM((KV_RANK, hq), F32),
            pltpu.VMEM((1, hq), F32),
            pltpu.VMEM((1, hq), F32),
            pltpu.VMEM((1, hq), F32),
            pltpu.VMEM((P2_CHUNK, hq), BF16),
        ],
        compiler_params=_params(("arbitrary", "arbitrary"), 56),
        name="dsa_attention",
    )(qbi, qbi, latent, kidx, ckv, ckvt, wuk_t, wuv_h, tab, tri)


def _merge_kernel(u_ref, oa_ref, ob_ref, wga_ref, wgb_ref, wpa_ref, wpb_ref, y_ref):
    u = u_ref[...]
    ga = jnp.dot(u, wga_ref[...], preferred_element_type=F32)
    gb = jnp.dot(u, wgb_ref[...], preferred_element_type=F32)
    pa = jnp.dot(oa_ref[...], wpa_ref[...], preferred_element_type=F32)
    pb = jnp.dot(ob_ref[...], wpb_ref[...], preferred_element_type=F32)
    y_ref[...] = (jax.nn.sigmoid(ga) * pa + jax.nn.sigmoid(gb) * pb).astype(y_ref.dtype)


def gated_merge(u, o_a, o_b, w_ga, w_gb, w_pa, w_pb, tm=512, tn=512):
    m, d = u.shape
    row = lambda i, j: (i, 0)
    col = lambda i, j: (0, j)
    return pl.pallas_call(
        _merge_kernel,
        out_shape=jax.ShapeDtypeStruct((m, d), BF16),
        grid=(m // tm, d // tn),
        in_specs=[pl.BlockSpec((tm, d), row), pl.BlockSpec((tm, A_WIDTH), row), pl.BlockSpec((tm, B_WIDTH), row),
                  pl.BlockSpec((d, tn), col), pl.BlockSpec((d, tn), col),
                  pl.BlockSpec((A_WIDTH, tn), col), pl.BlockSpec((B_WIDTH, tn), col)],
        out_specs=pl.BlockSpec((tm, tn), lambda i, j: (i, j)),
        compiler_params=_params(("parallel", "arbitrary"), 56),
        name="gated_merge",
    )(u, o_a, o_b, w_ga, w_gb, w_pa, w_pb)


def _xattn_kernel(x_ref, gx_ref, wq_ref, kv_ref, wo_ref, gf_ref, wr_ref, x2_ref, u3_ref, rl_ref):
    x = x_ref[...]
    u = _rms(x, gx_ref[...]).astype(BF16)
    q = jnp.dot(u, wq_ref[...], preferred_element_type=F32).astype(BF16)
    outs = []
    for h in range(MEM_HEADS):
        hs = slice(h * HEAD_DIM, (h + 1) * HEAD_DIM)
        lg = lax.dot_general(q[:, hs], kv_ref[:, hs], NT_DIMS, preferred_element_type=F32) * (HEAD_DIM ** -0.5)
        p = jnp.exp(lg - jnp.max(lg, axis=-1, keepdims=True))
        p = p * (1.0 / jnp.sum(p, axis=-1, keepdims=True))
        v = kv_ref[:, MEM_WIDTH + h * HEAD_DIM:MEM_WIDTH + (h + 1) * HEAD_DIM]
        outs.append(jnp.dot(p.astype(BF16), v, preferred_element_type=F32).astype(BF16))
    o = jnp.concatenate(outs, axis=1)
    x2 = x + jnp.dot(o, wo_ref[...], preferred_element_type=F32)
    x2_ref[...] = x2
    u3 = _rms(x2, gf_ref[...]).astype(BF16)
    u3_ref[...] = _pack_bf16_pairs(u3)
    rl_ref[...] = jnp.dot(u3, wr_ref[...], preferred_element_type=F32)


def _pack_bf16_pairs(a):
    k = a.shape[1] // 2
    lo = lax.bitcast_convert_type(a[:, :k].astype(F32), jnp.int32)
    hi = lax.bitcast_convert_type(a[:, k:].astype(F32), jnp.int32)
    return ((lo >> 16) & 0xFFFF) | (hi & jnp.int32(-65536))


def _unpack_bf16_pairs(w):
    lo = lax.bitcast_convert_type(w << 16, F32).astype(BF16)
    hi = lax.bitcast_convert_type(w & jnp.int32(-65536), F32).astype(BF16)
    return lo, hi


def cross_attention(x, g_x, w_qm, kv, w_om, g_ffn, w_router, bsz, s, tm=256):
    m, d = x.shape
    per = s // tm
    n_mem = kv.shape[1]
    const = lambda i: (0, 0)
    return pl.pallas_call(
        _xattn_kernel,
        out_shape=(jax.ShapeDtypeStruct((m, d), F32), jax.ShapeDtypeStruct((m, d // 2), jnp.int32),
                   jax.ShapeDtypeStruct((m, ROUTER_WIDTH), F32)),
        grid=(m // tm,),
        in_specs=[pl.BlockSpec((tm, d), lambda i: (i, 0)), pl.BlockSpec((1, d), const),
                  _resident((d, MEM_WIDTH), const),
                  pl.BlockSpec((None, n_mem, 2 * MEM_WIDTH), lambda i: (i // per, 0, 0)),
                  _resident((MEM_WIDTH, d), const), pl.BlockSpec((1, d), const),
                  _resident((d, ROUTER_WIDTH), const)],
        out_specs=(pl.BlockSpec((tm, d), lambda i: (i, 0)), pl.BlockSpec((tm, d // 2), lambda i: (i, 0)),
                   pl.BlockSpec((tm, ROUTER_WIDTH), lambda i: (i, 0))),
        compiler_params=_params(("parallel",), 56),
        name="cross_attention",
    )(x, g_x.reshape(1, d), w_qm, kv, w_om, g_ffn.reshape(1, d), w_router)


def _moe_kernel(be_ref, nu_ref, nr_ref, tok_ref, tok_next_ref, dst_ref, u_hbm, w1_ref, w3_ref, w2_ref, y_hbm,
                hbuf, h_scr, y_scr, w1b, w3b, w2b, gsem, ssem):
    i = pl.program_id(0)
    c = pl.program_id(1)
    n_c = EXPERT_FF // FF_CHUNK
    n_used = nu_ref[0]
    used = i < n_used

    def gather_row(tok, r):
        return pltpu.make_async_copy(u_hbm.at[pl.ds(tok, 1), :], hbuf.at[pl.ds(r, 1), :], gsem)

    def scatter_row(r, dst):
        return pltpu.make_async_copy(y_scr.at[pl.ds(r, 1), :], y_hbm.at[pl.ds(dst, 1), :], ssem)

    def wait_all_gathers():
        def body(r, carry):
            gather_row(0, r).wait()
            return carry
        lax.fori_loop(0, MOE_ROWS, body, 0, unroll=8)

    def wait_scatter(count):
        def body(r, carry):
            scatter_row(r, 0).wait()
            return carry
        lax.fori_loop(0, count, body, 0)

    @pl.when(used & (c == 0))
    def _():
        @pl.when(i == 0)
        def _():
            y_scr[...] = jnp.zeros(y_scr.shape, y_scr.dtype)

            def body(r, carry):
                gather_row(tok_ref[0, r], r).start()
                return carry
            lax.fori_loop(0, MOE_ROWS, body, 0)

        wait_all_gathers()
        half = h_scr.shape[1] // 2
        piece = 128
        for r0 in range(0, MOE_ROWS, piece):
            lo, hi = _unpack_bf16_pairs(hbuf[r0:r0 + piece, :])
            h_scr[r0:r0 + piece, :half] = lo
            h_scr[r0:r0 + piece, half:] = hi

    @pl.when(used & (c < n_c - 1))
    def _():
        base = n_c * MOE_GATHER_MAIN + c * MOE_GATHER_REST
        for j in range(MOE_GATHER_REST):
            gather_row(tok_next_ref[0, base + j], base + j).start()

    def sub_block(sb):
        rows = slice(sb * MOE_SUB, (sb + 1) * MOE_SUB)
        h = h_scr[rows, :]
        a = jnp.dot(h, w1b[...], preferred_element_type=F32)
        b = jnp.dot(h, w3b[...], preferred_element_type=F32)
        g = (a * jax.nn.sigmoid(a) * b).astype(BF16)
        if sb == 0:
            @pl.when((c == 0) & (i > 0))
            def _():
                wait_scatter(nr_ref[jnp.maximum(i - 1, 0)])
        y = jnp.dot(g, w2b[...], preferred_element_type=F32)
        y_scr[rows, :] = jnp.where(c == 0, y, y_scr[rows, :] + y)

    @pl.when(used)
    def _():
        w1b[...] = w1_ref[...].astype(BF16)
        w3b[...] = w3_ref[...].astype(BF16)
        w2b[...] = w2_ref[...].astype(BF16)
        base = c * MOE_GATHER_MAIN
        for j in range(MOE_GATHER_MAIN):
            gather_row(tok_next_ref[0, base + j], base + j).start()
        sub_block(0)
        for sb in range(1, MOE_ROWS // MOE_SUB):
            pl.when(nr_ref[i] > sb * MOE_SUB)(functools.partial(sub_block, sb))

        @pl.when(c == n_c - 1)
        def _():
            def body(r, carry):
                scatter_row(r, dst_ref[0, r]).start()
                return carry
            lax.fori_loop(0, nr_ref[i], body, 0)

            @pl.when(i == n_used - 1)
            def _():
                wait_scatter(nr_ref[i])
                wait_all_gathers()


def moe_ffn(u_packed, slot_tok, slot_dst, blk_e, n_used, n_real, w1, w3, w2):
    n, dh = u_packed.shape
    d = 2 * dh
    n_blk = slot_tok.shape[0]
    n_c = EXPERT_FF // FF_CHUNK

    def blk(i, nu):
        return jnp.minimum(i, nu[0] - 1)

    def up_map(i, c, be, nu, nr):
        return (be[blk(i, nu)], 0, jnp.where(i < nu[0], c, n_c - 1))

    def down_map(i, c, be, nu, nr):
        return (be[blk(i, nu)], jnp.where(i < nu[0], c, n_c - 1), 0)

    def slots(shift):
        return pl.BlockSpec((None, 1, MOE_ROWS), lambda i, c, be, nu, nr: (jnp.minimum(i + shift, nu[0] - 1), 0, 0),
                            memory_space=pltpu.SMEM)

    grid_spec = pltpu.PrefetchScalarGridSpec(
        num_scalar_prefetch=3,
        grid=(n_blk, n_c),
        in_specs=[slots(0), slots(1), slots(0), pl.BlockSpec(memory_space=pl.ANY),
                  pl.BlockSpec((None, d, FF_CHUNK), up_map), pl.BlockSpec((None, d, FF_CHUNK), up_map),
                  pl.BlockSpec((None, FF_CHUNK, d), down_map)],
        out_specs=pl.BlockSpec(memory_space=pl.ANY),
        scratch_shapes=[pltpu.VMEM((MOE_ROWS, dh), jnp.int32), pltpu.VMEM((MOE_ROWS, d), BF16),
                        pltpu.VMEM((MOE_ROWS, d), F32),
                        pltpu.VMEM((d, FF_CHUNK), BF16), pltpu.VMEM((d, FF_CHUNK), BF16), pltpu.VMEM((FF_CHUNK, d), BF16),
                        pltpu.SemaphoreType.DMA(()), pltpu.SemaphoreType.DMA(())],
    )
    return pl.pallas_call(
        _moe_kernel,
        out_shape=jax.ShapeDtypeStruct((EXPERT_TOPK * n, d), F32),
        grid_spec=grid_spec,
        compiler_params=_params(("arbitrary", "arbitrary"), 58),
        name="moe_ffn",
    )(blk_e, n_used, n_real, slot_tok, slot_tok, slot_dst, u_packed, w1, w3, w2)


def _combine_kernel(x_ref, y1_ref, y2_ref, w_ref, g_ref, o_ref):
    w = w_ref[...]
    x3 = x_ref[...] + (y1_ref[...] * w[:, 0:1] + y2_ref[...] * w[:, 1:2])
    o_ref[...] = _rms(x3, g_ref[...])


def combine_norm(x, y, w, g, tm=256):
    m, d = x.shape
    row = lambda i: (i, 0)
    return pl.pallas_call(
        _combine_kernel,
        out_shape=jax.ShapeDtypeStruct((m, d), F32),
        grid=(m // tm,),
        in_specs=[pl.BlockSpec((tm, d), row), pl.BlockSpec((None, tm, d), lambda i: (0, i, 0)),
                  pl.BlockSpec((None, tm, d), lambda i: (1, i, 0)),
                  pl.BlockSpec((tm, EXPERT_TOPK), row), pl.BlockSpec((1, d), lambda i: (0, 0))],
        out_specs=pl.BlockSpec((tm, d), row),
        compiler_params=_params(("parallel",), 48),
        name="combine_norm",
    )(x, y, y, w, g.reshape(1, d))


def _rel_bucket(dist):
    max_exact = N_BUCKETS // 2
    d = jnp.maximum(dist, 0)
    df = jnp.maximum(d, 1).astype(F32)
    large = max_exact + (jnp.log(df / max_exact) / math.log(MAX_DISTANCE / max_exact)
                         * (N_BUCKETS - max_exact)).astype(jnp.int32)
    large = jnp.minimum(large, N_BUCKETS - 1)
    return jnp.where(d < max_exact, d, large)


def _swa_bias(bias_table):
    qi = jnp.arange(WINDOW, dtype=jnp.int32)[:, None]
    sj = jnp.arange(2 * WINDOW, dtype=jnp.int32)[None, :]
    dist = qi + WINDOW - sj
    bias = jnp.transpose(bias_table[_rel_bucket(dist)], (2, 0, 1)).astype(F32)
    bias = jnp.where(((dist >= 0) & (dist < WINDOW))[None], bias, -jnp.inf)
    return bias.reshape(A_KV_HEADS, A_GROUP * WINDOW, 2 * WINDOW)


def _dsa_bias(bias_table):
    kk = jnp.arange(Q_BLOCK, dtype=jnp.int32)[:, None]
    qq = jnp.arange(Q_BLOCK, dtype=jnp.int32)[None, :]
    tabs = []
    for dist in (Q_BLOCK + qq - kk, qq - kk):
        b = bias_table[_rel_bucket(dist)].astype(F32) - bias_table[N_BUCKETS - 1].astype(F32)
        tabs.append(jnp.transpose(b, (0, 2, 1)).reshape(Q_BLOCK, B_HEADS * Q_BLOCK))
    return jnp.stack(tabs)


def _moe_routing(rl, b_grp, b_exp):
    n = rl.shape[0]
    grp_logits = rl[:, :N_GROUPS] + b_grp.astype(F32)
    grp_p = jax.nn.softmax(grp_logits, axis=-1)
    g_top = jnp.argmax(grp_logits, axis=-1).astype(jnp.int32)
    g_gate = jnp.take_along_axis(grp_p, g_top[:, None], axis=1)
    exp_logits = (rl[:, N_GROUPS:N_GROUPS + N_EXPERTS] + b_exp.astype(F32)).reshape(n, N_GROUPS, EXPERTS_PER_GROUP)
    in_grp = jnp.take_along_axis(exp_logits, g_top[:, None, None], axis=1)[:, 0]
    top_v, top_j = lax.top_k(in_grp, EXPERT_TOPK)
    gate = g_gate * jax.nn.softmax(top_v, axis=-1)
    eid = (g_top[:, None] * EXPERTS_PER_GROUP + top_j).reshape(-1).astype(jnp.int32)
    n_assign = n * EXPERT_TOPK
    tok_f = jnp.repeat(jnp.arange(n, dtype=jnp.int32), EXPERT_TOPK)
    order = jnp.argsort(eid)
    se = eid[order]
    counts = jnp.bincount(eid, length=N_EXPERTS).astype(jnp.int32)
    padded = ((counts + MOE_ROWS - 1) // MOE_ROWS) * MOE_ROWS
    pad_end = jnp.cumsum(padded)
    pad_start = pad_end - padded
    start = jnp.cumsum(counts) - counts
    dest = pad_start[se] + (jnp.arange(n_assign, dtype=jnp.int32) - start[se])
    n_slots = n_assign + N_EXPERTS * MOE_ROWS
    n_blk = n_slots // MOE_ROWS
    slot_tok = jnp.zeros((n_slots,), jnp.int32).at[dest].set(tok_f[order])
    slot_dst = jnp.zeros((n_slots,), jnp.int32).at[dest].set((order % EXPERT_TOPK) * n + order // EXPERT_TOPK)
    blk_start = jnp.arange(n_blk, dtype=jnp.int32) * MOE_ROWS
    blk_e = jnp.minimum(jnp.searchsorted(pad_end, blk_start, side='right'), N_EXPERTS - 1).astype(jnp.int32)
    n_real = jnp.clip(counts[blk_e] - (blk_start - pad_start[blk_e]), 0, MOE_ROWS).astype(jnp.int32)
    n_used = (pad_end[-1:] // MOE_ROWS).astype(jnp.int32)
    return (gate, slot_tok.reshape(n_blk, 1, MOE_ROWS), slot_dst.reshape(n_blk, 1, MOE_ROWS), blk_e, n_used, n_real)


def _layer(x, mem, rel_bias, g_mix, w_in, g_cq, w_uq, w_qidx, g_ckv, w_uk, w_uv, g_kidx, sink_a, w_pa, w_pb, w_out,
           g_xattn, g_mem, w_qm, w_km, w_vm, w_om, g_ffn, w_grp, b_grp, w_exp, b_exp, w_e1, w_e3, w_e2):
    bsz, s, d = x.shape
    n = bsz * s
    xt = x.reshape(n, d)
    c_qkv = A_WIDTH + 2 * KV_WIDTH
    c_lat = c_qkv + Q_RANK + KV_RANK + IDX_DIM + IDX_HEADS
    w_qkv = w_in[:, :c_qkv].astype(BF16)
    w_lat = jnp.pad(w_in[:, c_qkv:c_lat], ((0, 0), (0, LATENT_WIDTH - (c_lat - c_qkv)))).astype(BF16)
    w_ga = w_in[:, c_lat:c_lat + d].astype(BF16)
    w_gb = w_in[:, c_lat + d:].astype(BF16)

    u = rmsnorm_cast(xt, g_mix)
    qkv = matmul(u, w_qkv, BF16)
    latent = matmul(u, w_lat, F32)
    o_a = swa_attention(qkv, sink_a.astype(F32), _swa_bias(rel_bias[:, :A_HEADS]), bsz, s)

    ckv, ckvt, kidx = latent_norm(latent, g_ckv, g_kidx, bsz, s)
    qbi = rms_matmul_heads(latent, g_cq, jnp.concatenate([w_uq, w_qidx], axis=1).astype(BF16))
    tri = jnp.tril(jnp.ones((P2_CHUNK, P2_CHUNK), BF16))
    o_b = dsa_attention(qbi, latent, kidx, ckv, ckvt, jnp.transpose(w_uk, (1, 2, 0)).astype(BF16),
                        jnp.transpose(w_uv, (1, 0, 2)).astype(BF16), _dsa_bias(rel_bias[:, A_HEADS:]), tri, bsz, s)

    y = gated_merge(u, o_a, o_b, w_ga, w_gb, w_pa.astype(BF16), w_pb.astype(BF16))
    x1 = matmul_residual(y, w_out.astype(BF16), xt)

    n_mem = mem.shape[1]
    um = rmsnorm_cast(mem.reshape(bsz * n_mem, d), g_mem)
    kv = matmul(um, jnp.concatenate([w_km, w_vm], axis=1).astype(BF16), BF16, tm=bsz * n_mem)
    w_router = jnp.pad(jnp.concatenate([w_grp, w_exp], axis=1),
                       ((0, 0), (0, ROUTER_WIDTH - N_GROUPS - N_EXPERTS))).astype(BF16)
    x2, u3, rl = cross_attention(x1, g_xattn, w_qm.astype(BF16), kv.reshape(bsz, n_mem, 2 * MEM_WIDTH),
                                 w_om.astype(BF16), g_ffn, w_router, bsz, s)

    gate, slot_tok, slot_dst, blk_e, n_used, n_real = _moe_routing(rl, b_grp, b_exp)
    ys = moe_ffn(u3, slot_tok, slot_dst, blk_e, n_used, n_real, w_e1, w_e3, w_e2)
    return x2, ys.reshape(EXPERT_TOPK, n, d), gate


def kernel(x, mem, rel_bias, g_mix, w_in, g_cq, w_uq, w_qidx, g_ckv, w_uk, w_uv, g_kidx, sink_a, w_pa, w_pb, w_out,
           g_xattn, g_mem, w_qm, w_km, w_vm, w_om, g_ffn, w_grp, b_grp, w_exp, b_exp, w_e1, w_e3, w_e2, g_final):
    assert g_mix.shape[0] == 1, "one layer"
    bsz, s, d = x.shape
    x2, ys, gate = _layer(x, mem, rel_bias, g_mix[0], w_in[0], g_cq[0], w_uq[0], w_qidx[0], g_ckv[0], w_uk[0],
                              w_uv[0], g_kidx[0], sink_a[0], w_pa[0], w_pb[0], w_out[0], g_xattn[0], g_mem[0],
                              w_qm[0], w_km[0], w_vm[0], w_om[0], g_ffn[0], w_grp[0], b_grp[0], w_exp[0], b_exp[0],
                              w_e1[0], w_e3[0], w_e2[0])
    out = combine_norm(x2, ys, gate, g_final)
    return out.reshape(bsz, s, d)
```

```python
import functools
import math

import numpy as np
import jax
import jax.numpy as jnp
from jax import lax
from jax.experimental import pallas as pl
from jax.experimental.pallas import tpu as pltpu

F32 = jnp.float32
BF16 = jnp.bfloat16

HEAD_DIM = 128
A_HEADS = 16
A_KV_HEADS = 4
A_GROUP = A_HEADS // A_KV_HEADS
WINDOW = 128
A_WIDTH = A_HEADS * HEAD_DIM
KV_WIDTH = A_KV_HEADS * HEAD_DIM
B_HEADS = 16
B_WIDTH = B_HEADS * HEAD_DIM
Q_RANK = 1024
KV_RANK = 512
IDX_HEADS = 16
IDX_DIM = 128
TOPK_KEYS = 256
Q_BLOCK = 128
N_BUCKETS = 32
MAX_DISTANCE = 128
MEM_HEADS = 4
MEM_WIDTH = MEM_HEADS * HEAD_DIM
N_GROUPS = 8
EXPERTS_PER_GROUP = 8
N_EXPERTS = N_GROUPS * EXPERTS_PER_GROUP
EXPERT_FF = 768
EXPERT_TOPK = 2
EPS = 1e-6

LANE = 128
LATENT_WIDTH = 2048
W_IDX_COL = Q_RANK + KV_RANK + IDX_DIM
MOE_ROWS = 640
MOE_PIECES = ((0, 256), (256, 256), (512, 128))
MOE_STEPS = 8
MOE_K = 512
MOE_F = EXPERT_FF // MOE_STEPS
assert sum(size for _, size in MOE_PIECES) == MOE_ROWS and EXPERT_FF % MOE_STEPS == 0 and MOE_F % 16 == 0
ROUTER_WIDTH = 128
INT_MIN = -(2 ** 31)
KEY_NEG_INF = int(np.array([0xFF800000 ^ 0x7FFFFFFF], np.uint32).view(np.int32)[0])
NEG_BIG = -1e30
NT_DIMS = (((1,), (1,)), ((), ()))


def _params(semantics, vmem_mb):
    return pltpu.CompilerParams(dimension_semantics=semantics, vmem_limit_bytes=int(vmem_mb * 2 ** 20))


def _rms(xf, g):
    return xf * lax.rsqrt(jnp.mean(xf * xf, axis=-1, keepdims=True) + EPS) * g


def _resident(block, index_map):
    return pl.BlockSpec(block, index_map, pipeline_mode=pl.Buffered(1))


def _rmsnorm_kernel(x_ref, g_ref, o_ref):
    o_ref[...] = _rms(x_ref[...], g_ref[...]).astype(o_ref.dtype)


def rmsnorm_cast(x, g, tm=512):
    m, k = x.shape
    return pl.pallas_call(
        _rmsnorm_kernel,
        out_shape=jax.ShapeDtypeStruct((m, k), BF16),
        grid=(m // tm,),
        in_specs=[pl.BlockSpec((tm, k), lambda i: (i, 0)), pl.BlockSpec((1, k), lambda i: (0, 0))],
        out_specs=pl.BlockSpec((tm, k), lambda i: (i, 0)),
        compiler_params=_params(("parallel",), 40),
        name="rmsnorm_cast",
    )(x, g.reshape(1, k))


def _mm_kernel(a_ref, w_ref, o_ref):
    o_ref[...] = jnp.dot(a_ref[...], w_ref[...], preferred_element_type=F32).astype(o_ref.dtype)


def matmul(a, w, out_dtype, tm=1024, tn=512):
    m, k = a.shape
    n = w.shape[1]
    return pl.pallas_call(
        _mm_kernel,
        out_shape=jax.ShapeDtypeStruct((m, n), out_dtype),
        grid=(m // tm, n // tn),
        in_specs=[pl.BlockSpec((tm, k), lambda i, j: (i, 0)), pl.BlockSpec((k, tn), lambda i, j: (0, j))],
        out_specs=pl.BlockSpec((tm, tn), lambda i, j: (i, j)),
        compiler_params=_params(("parallel", "arbitrary"), 48),
        name="matmul",
    )(a, w)


def _mm_res_kernel(a_ref, w_ref, r_ref, o_ref):
    o_ref[...] = r_ref[...] + jnp.dot(a_ref[...], w_ref[...], preferred_element_type=F32)


def matmul_residual(a, w, res, tm=1024, tn=512):
    m, k = a.shape
    n = w.shape[1]
    return pl.pallas_call(
        _mm_res_kernel,
        out_shape=jax.ShapeDtypeStruct((m, n), F32),
        grid=(m // tm, n // tn),
        in_specs=[pl.BlockSpec((tm, k), lambda i, j: (i, 0)), pl.BlockSpec((k, tn), lambda i, j: (0, j)),
                  pl.BlockSpec((tm, tn), lambda i, j: (i, j))],
        out_specs=pl.BlockSpec((tm, tn), lambda i, j: (i, j)),
        compiler_params=_params(("parallel", "arbitrary"), 48),
        name="matmul_residual",
    )(a, w, res)


def _rms_mm_heads_kernel(x_ref, g_ref, w_ref, o_ref, u_ref):
    @pl.when(pl.program_id(1) == 0)
    def _():
        u_ref[...] = _rms(x_ref[...], g_ref[...]).astype(BF16)

    r = jnp.dot(u_ref[...], w_ref[...], preferred_element_type=F32)
    for hh in range(o_ref.shape[0]):
        o_ref[hh] = r[:, hh * LANE:(hh + 1) * LANE].astype(o_ref.dtype)


def rms_matmul_heads(x, g, w, tm=1024, tn=512):
    m = x.shape[0]
    k, n = w.shape
    hb = tn // LANE
    return pl.pallas_call(
        _rms_mm_heads_kernel,
        out_shape=jax.ShapeDtypeStruct((n // LANE, m, LANE), BF16),
        grid=(m // tm, n // tn),
        in_specs=[pl.BlockSpec((tm, k), lambda i, j: (i, 0)), pl.BlockSpec((1, k), lambda i, j: (0, 0)),
                  pl.BlockSpec((k, tn), lambda i, j: (0, j))],
        out_specs=pl.BlockSpec((hb, tm, LANE), lambda i, j: (j, i, 0)),
        scratch_shapes=[pltpu.VMEM((tm, k), BF16)],
        compiler_params=_params(("parallel", "arbitrary"), 40),
        name="rms_matmul_heads",
    )(x, g.reshape(1, k), w)


def _swa_kernel(sink_ref, q_ref, kp_ref, kc_ref, vp_ref, vc_ref, bias_ref, o_ref):
    n = pl.program_id(1)
    kb = jnp.concatenate([kp_ref[...], kc_ref[...]], axis=0)
    vb = jnp.concatenate([vp_ref[...], vc_ref[...]], axis=0)
    col = lax.broadcasted_iota(jnp.int32, (1, 2 * WINDOW), 1)
    no_prev = jnp.where((col < WINDOW) & (n == 0), -jnp.inf, 0.0)
    for kh in range(A_KV_HEADS):
        heads = [kh * A_GROUP + g for g in range(A_GROUP)]
        qs = jnp.concatenate([q_ref[:, h * HEAD_DIM:(h + 1) * HEAD_DIM] for h in heads], axis=0)
        k = kb[:, kh * HEAD_DIM:(kh + 1) * HEAD_DIM]
        v = vb[:, kh * HEAD_DIM:(kh + 1) * HEAD_DIM]
        lg = lax.dot_general(qs, k, NT_DIMS, preferred_element_type=F32) * (HEAD_DIM ** -0.5)
        lg = lg + bias_ref[kh] + no_prev
        sk = jnp.concatenate([jnp.full((WINDOW, 1), sink_ref[h], F32) for h in heads], axis=0)
        mx = jnp.maximum(jnp.max(lg, axis=-1, keepdims=True), sk)
        p = jnp.exp(lg - mx)
        den = jnp.sum(p, axis=-1, keepdims=True) + jnp.exp(sk - mx)
        o = jnp.dot((p * (1.0 / den)).astype(BF16), v, preferred_element_type=F32)
        for g, h in enumerate(heads):
            o_ref[:, h * HEAD_DIM:(h + 1) * HEAD_DIM] = o[g * WINDOW:(g + 1) * WINDOW].astype(o_ref.dtype)


def swa_attention(qkv, sink, bias, bsz, s):
    nb = s // WINDOW
    kcol = A_WIDTH // KV_WIDTH
    cur = lambda b, n: b * nb + n
    prev = lambda b, n: b * nb + jnp.maximum(n - 1, 0)
    return pl.pallas_call(
        _swa_kernel,
        out_shape=jax.ShapeDtypeStruct((bsz * s, A_WIDTH), BF16),
        grid=(bsz, nb),
        in_specs=[
            pl.BlockSpec(memory_space=pltpu.SMEM),
            pl.BlockSpec((WINDOW, A_WIDTH), lambda b, n: (cur(b, n), 0)),
            pl.BlockSpec((WINDOW, KV_WIDTH), lambda b, n: (prev(b, n), kcol)),
            pl.BlockSpec((WINDOW, KV_WIDTH), lambda b, n: (cur(b, n), kcol)),
            pl.BlockSpec((WINDOW, KV_WIDTH), lambda b, n: (prev(b, n), kcol + 1)),
            pl.BlockSpec((WINDOW, KV_WIDTH), lambda b, n: (cur(b, n), kcol + 1)),
            pl.BlockSpec((A_KV_HEADS, A_GROUP * WINDOW, 2 * WINDOW), lambda b, n: (0, 0, 0)),
        ],
        out_specs=pl.BlockSpec((WINDOW, A_WIDTH), lambda b, n: (cur(b, n), 0)),
        compiler_params=_params(("parallel", "arbitrary"), 32),
        name="swa_attention",
    )(sink, qkv, qkv, qkv, qkv, qkv, bias)


def _latent_norm_kernel(x_ref, gkv_ref, gk_ref, ckv_ref, ckvt_ref, kidx_ref):
    c = _rms(x_ref[:, :KV_RANK], gkv_ref[...])
    ckv_ref[...] = c.astype(BF16)
    ckvt_ref[...] = c.T.astype(BF16)
    kidx_ref[...] = _rms(x_ref[:, KV_RANK:KV_RANK + IDX_DIM], gk_ref[...]).astype(BF16)


def latent_norm(latent, g_ckv, g_kidx, bsz, s, tm=512):
    per = s // tm
    half = LATENT_WIDTH // 2
    return pl.pallas_call(
        _latent_norm_kernel,
        out_shape=(jax.ShapeDtypeStruct((bsz, s, KV_RANK), BF16), jax.ShapeDtypeStruct((bsz, KV_RANK, s), BF16),
                   jax.ShapeDtypeStruct((bsz, s, IDX_DIM), BF16)),
        grid=(bsz * per,),
        in_specs=[pl.BlockSpec((tm, half), lambda i: (i, 1)), pl.BlockSpec((1, KV_RANK), lambda i: (0, 0)),
                  pl.BlockSpec((1, IDX_DIM), lambda i: (0, 0))],
        out_specs=(pl.BlockSpec((None, tm, KV_RANK), lambda i: (i // per, i % per, 0)),
                   pl.BlockSpec((None, KV_RANK, tm), lambda i: (i // per, 0, i % per)),
                   pl.BlockSpec((None, tm, IDX_DIM), lambda i: (i // per, i % per, 0))),
        compiler_params=_params(("parallel",), 32),
        name="latent_norm",
    )(latent, g_ckv.reshape(1, KV_RANK), g_kidx.reshape(1, IDX_DIM))


P1_CHUNK = 1024
P2_CHUNK = 256


def _sortable_key(x):
    bits = lax.bitcast_convert_type(x, jnp.int32)
    return bits ^ ((bits >> 31) & 0x7FFFFFFF)


def _dsa_kernel(qb_ref, qi_ref, w_ref, kidx_ref, ckv_ref, ckvt_ref, wuk_ref, wuv_ref, tab_ref, tri_ref, o_ref,
                key_ref, qlat_ref, acc_ref, m_ref, l_ref, alpha_ref, p_ref):
    n = pl.program_id(1)
    nheads = B_HEADS
    lane_q = lax.broadcasted_iota(jnp.int32, (1, Q_BLOCK), 1)
    t_row = n * Q_BLOCK + lane_q

    for h in range(nheads):
        ql = jnp.dot(qb_ref[h], wuk_ref[h], preferred_element_type=F32) * (HEAD_DIM ** -0.5)
        qlat_ref[h * Q_BLOCK:(h + 1) * Q_BLOCK, :] = ql.astype(BF16)

    w_t = w_ref[...].T

    n_p1 = (n * Q_BLOCK + Q_BLOCK + P1_CHUNK - 1) // P1_CHUNK

    def p1_body(c, carry):
        start = pl.multiple_of(c * P1_CHUNK, P1_CHUNK)
        kc = kidx_ref[pl.ds(start, P1_CHUNK), :]
        score = jnp.zeros((P1_CHUNK, Q_BLOCK), F32)
        hg = 4
        for g0 in range(0, IDX_HEADS, hg):
            q4 = qi_ref[g0:g0 + hg].reshape(hg * Q_BLOCK, IDX_DIM)
            st = lax.dot_general(kc, q4, NT_DIMS, preferred_element_type=F32)
            for j in range(hg):
                h = g0 + j
                score = score + jnp.maximum(st[:, j * Q_BLOCK:(j + 1) * Q_BLOCK], 0.0) * w_t[h:h + 1, :]
        score = score * (IDX_DIM ** -0.5 * IDX_HEADS ** -0.5)
        s_idx = start + lax.broadcasted_iota(jnp.int32, (P1_CHUNK, Q_BLOCK), 0)
        score = jnp.where(s_idx <= t_row, score, -jnp.inf)
        key_ref[pl.ds(start, P1_CHUNK), :] = _sortable_key(score)
        return carry

    lax.fori_loop(0, n_p1, p1_body, 0)

    def count(pred_fn):
        def body(c, cnt8):
            sub = 512
            for j in range(P1_CHUNK // sub):
                blk = key_ref[pl.ds(pl.multiple_of(c * P1_CHUNK + j * sub, sub), sub), :]
                hit = jnp.where(pred_fn(blk), 1, 0).astype(jnp.int32)
                cnt8 = cnt8 + hit.reshape(sub // 8, 8, Q_BLOCK).sum(axis=0)
            return cnt8

        cnt8 = lax.fori_loop(0, n_p1, body, jnp.zeros((8, Q_BLOCK), jnp.int32))
        return cnt8.sum(axis=0, keepdims=True)

    def bit_body(i, thr):
        cand = thr + jnp.left_shift(jnp.int32(1), 31 - i)
        return jnp.where(count(lambda blk: blk >= cand) >= TOPK_KEYS, cand, thr)

    thr = lax.fori_loop(0, 32, bit_body, jnp.full((1, Q_BLOCK), INT_MIN, jnp.int32))
    need = (TOPK_KEYS - count(lambda blk: blk > thr)).astype(F32)

    m_ref[...] = jnp.full(m_ref.shape, NEG_BIG, F32)
    l_ref[...] = jnp.zeros(l_ref.shape, F32)
    acc_ref[...] = jnp.zeros(acc_ref.shape, F32)

    def chunk(c, eq_seen, near):
        start = pl.multiple_of(c * P2_CHUNK, P2_CHUNK)
        u = key_ref[pl.ds(start, P2_CHUNK), :]
        eq = u == thr
        pref = jnp.dot(tri_ref[...], jnp.where(eq, 1.0, 0.0).astype(BF16), preferred_element_type=F32)
        sel = ((u > thr) | (eq & (eq_seen + pref <= need))) & (u != KEY_NEG_INF)
        mask = jnp.where(sel, 0.0, -jnp.inf)
        eq_seen = eq_seen + pref[P2_CHUNK - 1:P2_CHUNK, :]
        lg = lax.dot_general(ckv_ref[pl.ds(start, P2_CHUNK), :], qlat_ref[...], NT_DIMS,
                             preferred_element_type=F32)
        if near:
            halves = []
            for half in range(P2_CHUNK // Q_BLOCK):
                blk = c * (P2_CHUNK // Q_BLOCK) + half
                halves.append(jnp.where(blk == n, tab_ref[1], jnp.where(blk == n - 1, tab_ref[0], 0.0)))
        for h in range(nheads):
            hs = slice(h * Q_BLOCK, (h + 1) * Q_BLOCK)
            lh = lg[:, hs] + mask
            if near:
                lh = lh + jnp.concatenate([b[:, hs] for b in halves], axis=0)
            m_old = m_ref[:, hs]
            m_new = jnp.maximum(m_old, jnp.max(lh, axis=0, keepdims=True))
            alpha = jnp.exp(m_old - m_new)
            p = jnp.exp(lh - m_new)
            l_ref[:, hs] = alpha * l_ref[:, hs] + jnp.sum(p, axis=0, keepdims=True)
            m_ref[:, hs] = m_new
            alpha_ref[:, hs] = alpha
            p_ref[:, hs] = p.astype(BF16)
        pv = jnp.dot(ckvt_ref[:, pl.ds(start, P2_CHUNK)], p_ref[...], preferred_element_type=F32)
        acc_ref[...] = acc_ref[...] * alpha_ref[...] + pv
        return eq_seen

    per = P2_CHUNK // Q_BLOCK
    c_end = n // per + 1
    c_near = jnp.maximum(n - 1, 0) // per
    eq_seen = lax.fori_loop(0, c_near, functools.partial(chunk, near=False), jnp.zeros((1, Q_BLOCK), F32))
    lax.fori_loop(c_near, c_end, functools.partial(chunk, near=True), eq_seen)

    for h in range(nheads):
        hs = slice(h * Q_BLOCK, (h + 1) * Q_BLOCK)
        o_lat = (acc_ref[:, hs] * (1.0 / l_ref[:, hs])).T
        o = jnp.dot(o_lat.astype(BF16), wuv_ref[h], preferred_element_type=F32)
        o_ref[:, h * HEAD_DIM:(h + 1) * HEAD_DIM] = o.astype(o_ref.dtype)


def dsa_attention(qbi, latent, kidx, ckv, ckvt, wuk_t, wuv_h, tab, tri, bsz, s):
    nb = s // Q_BLOCK
    row = lambda b, n: b * nb + n
    hq = B_HEADS * Q_BLOCK
    return pl.pallas_call(
        _dsa_kernel,
        out_shape=jax.ShapeDtypeStruct((bsz * s, B_WIDTH), BF16),
        grid=(bsz, nb),
        in_specs=[
            pl.BlockSpec((B_HEADS, Q_BLOCK, HEAD_DIM), lambda b, n: (0, row(b, n), 0)),
            pl.BlockSpec((IDX_HEADS, Q_BLOCK, IDX_DIM), lambda b, n: (1, row(b, n), 0)),
            pl.BlockSpec((Q_BLOCK, LANE), lambda b, n: (row(b, n), W_IDX_COL // LANE)),
            _resident((None, s, IDX_DIM), lambda b, n: (b, 0, 0)),
            _resident((None, s, KV_RANK), lambda b, n: (b, 0, 0)),
            _resident((None, KV_RANK, s), lambda b, n: (b, 0, 0)),
            _resident((B_HEADS, HEAD_DIM, KV_RANK), lambda b, n: (0, 0, 0)),
            _resident((B_HEADS, KV_RANK, HEAD_DIM), lambda b, n: (0, 0, 0)),
            _resident((2, Q_BLOCK, hq), lambda b, n: (0, 0, 0)),
            _resident((P2_CHUNK, P2_CHUNK), lambda b, n: (0, 0)),
        ],
        out_specs=pl.BlockSpec((Q_BLOCK, B_WIDTH), lambda b, n: (row(b, n), 0)),
        scratch_shapes=[
            pltpu.VMEM((s, Q_BLOCK), jnp.int32),
            pltpu.VMEM((hq, KV_RANK), BF16),
            pltpu.VMEM((KV_RANK, hq), F32),
            pltpu.VMEM((1, hq), F32),
            pltpu.VMEM((1, hq), F32),
            pltpu.VMEM((1, hq), F32),
            pltpu.VMEM((P2_CHUNK, hq), BF16),
        ],
        compiler_params=_params(("arbitrary", "arbitrary"), 56),
        name="dsa_attention",
    )(qbi, qbi, latent, kidx, ckv, ckvt, wuk_t, wuv_h, tab, tri)


def _merge_kernel(u_ref, oa_ref, ob_ref, wga_ref, wgb_ref, wpa_ref, wpb_ref, y_ref):
    u = u_ref[...]
    ga = jnp.dot(u, wga_ref[...], preferred_element_type=F32)
    gb = jnp.dot(u, wgb_ref[...], preferred_element_type=F32)
    pa = jnp.dot(oa_ref[...], wpa_ref[...], preferred_element_type=F32)
    pb = jnp.dot(ob_ref[...], wpb_ref[...], preferred_element_type=F32)
    y_ref[...] = (jax.nn.sigmoid(ga) * pa + jax.nn.sigmoid(gb) * pb).astype(y_ref.dtype)


def gated_merge(u, o_a, o_b, w_ga, w_gb, w_pa, w_pb, tm=512, tn=512):
    m, d = u.shape
    row = lambda i, j: (i, 0)
    col = lambda i, j: (0, j)
    return pl.pallas_call(
        _merge_kernel,
        out_shape=jax.ShapeDtypeStruct((m, d), BF16),
        grid=(m // tm, d // tn),
        in_specs=[pl.BlockSpec((tm, d), row), pl.BlockSpec((tm, A_WIDTH), row), pl.BlockSpec((tm, B_WIDTH), row),
                  pl.BlockSpec((d, tn), col), pl.BlockSpec((d, tn), col),
                  pl.BlockSpec((A_WIDTH, tn), col), pl.BlockSpec((B_WIDTH, tn), col)],
        out_specs=pl.BlockSpec((tm, tn), lambda i, j: (i, j)),
        compiler_params=_params(("parallel", "arbitrary"), 56),
        name="gated_merge",
    )(u, o_a, o_b, w_ga, w_gb, w_pa, w_pb)


def _xattn_kernel(x_ref, gx_ref, wq_ref, kv_ref, wo_ref, gf_ref, wr_ref, x2_ref, u3_ref, rl_ref):
    x = x_ref[...]
    u = _rms(x, gx_ref[...]).astype(BF16)
    q = jnp.dot(u, wq_ref[...], preferred_element_type=F32).astype(BF16)
    outs = []
    for h in range(MEM_HEADS):
        hs = slice(h * HEAD_DIM, (h + 1) * HEAD_DIM)
        lg = lax.dot_general(q[:, hs], kv_ref[:, hs], NT_DIMS, preferred_element_type=F32) * (HEAD_DIM ** -0.5)
        p = jnp.exp(lg - jnp.max(lg, axis=-1, keepdims=True))
        p = p * (1.0 / jnp.sum(p, axis=-1, keepdims=True))
        v = kv_ref[:, MEM_WIDTH + h * HEAD_DIM:MEM_WIDTH + (h + 1) * HEAD_DIM]
        outs.append(jnp.dot(p.astype(BF16), v, preferred_element_type=F32).astype(BF16))
    o = jnp.concatenate(outs, axis=1)
    x2 = x + jnp.dot(o, wo_ref[...], preferred_element_type=F32)
    x2_ref[...] = x2
    u3 = _rms(x2, gf_ref[...]).astype(BF16)
    u3_ref[...] = _pack_bf16_pairs(u3)
    rl_ref[...] = jnp.dot(u3, wr_ref[...], preferred_element_type=F32)


def _pack_bf16_pairs(a):
    k = a.shape[1] // 2
    lo = lax.bitcast_convert_type(a[:, :k].astype(F32), jnp.int32)
    hi = lax.bitcast_convert_type(a[:, k:].astype(F32), jnp.int32)
    return ((lo >> 16) & 0xFFFF) | (hi & jnp.int32(-65536))


def _unpack_bf16_pairs(w):
    lo = lax.bitcast_convert_type(w << 16, F32).astype(BF16)
    hi = lax.bitcast_convert_type(w & jnp.int32(-65536), F32).astype(BF16)
    return lo, hi


def cross_attention(x, g_x, w_qm, kv, w_om, g_ffn, w_router, bsz, s, tm=256):
    m, d = x.shape
    per = s // tm
    n_mem = kv.shape[1]
    const = lambda i: (0, 0)
    return pl.pallas_call(
        _xattn_kernel,
        out_shape=(jax.ShapeDtypeStruct((m, d), F32), jax.ShapeDtypeStruct((m, d // 2), jnp.int32),
                   jax.ShapeDtypeStruct((m, ROUTER_WIDTH), F32)),
        grid=(m // tm,),
        in_specs=[pl.BlockSpec((tm, d), lambda i: (i, 0)), pl.BlockSpec((1, d), const),
                  _resident((d, MEM_WIDTH), const),
                  pl.BlockSpec((None, n_mem, 2 * MEM_WIDTH), lambda i: (i // per, 0, 0)),
                  _resident((MEM_WIDTH, d), const), pl.BlockSpec((1, d), const),
                  _resident((d, ROUTER_WIDTH), const)],
        out_specs=(pl.BlockSpec((tm, d), lambda i: (i, 0)), pl.BlockSpec((tm, d // 2), lambda i: (i, 0)),
                   pl.BlockSpec((tm, ROUTER_WIDTH), lambda i: (i, 0))),
        compiler_params=_params(("parallel",), 56),
        name="cross_attention",
    )(x, g_x.reshape(1, d), w_qm, kv, w_om, g_ffn.reshape(1, d), w_router)


def _moe_kernel(be_ref, nu_ref, nr_ref, tok_ref, tok_next_ref, dst_ref, u_hbm, w1_ref, w3_ref, w2_ref, y_hbm,
                hbuf, h_scr, y_scr, a_acc, b_acc, w1b, w3b, w2b, gsem, ssem):
    i = pl.program_id(0)
    k = pl.program_id(1)
    n_used = nu_ref[0]
    used = i < n_used
    nr = nr_ref[i]

    def gather_row(tok, r):
        return pltpu.make_async_copy(u_hbm.at[pl.ds(tok, 1), :], hbuf.at[pl.ds(r, 1), :], gsem)

    def scatter_row(r, dst):
        return pltpu.make_async_copy(y_scr.at[pl.ds(r, 1), :], y_hbm.at[pl.ds(dst, 1), :], ssem)

    def wait_all_gathers():
        def body(r, carry):
            gather_row(0, r).wait()
            return carry
        lax.fori_loop(0, MOE_ROWS, body, 0, unroll=8)

    def wait_scatter(count):
        def body(r, carry):
            scatter_row(r, 0).wait()
            return carry
        lax.fori_loop(0, count, body, 0)

    def for_pieces(fn):
        for p, (r0, size) in enumerate(MOE_PIECES):
            if p == 0:
                fn(slice(r0, r0 + size))
            else:
                pl.when(nr > r0)(functools.partial(fn, slice(r0, r0 + size)))

    @pl.when(used & (k == 0))
    def _():
        @pl.when(i == 0)
        def _():
            a_acc[...] = jnp.zeros(a_acc.shape, a_acc.dtype)
            b_acc[...] = jnp.zeros(b_acc.shape, b_acc.dtype)

            def body(r, carry):
                gather_row(tok_ref[0, r], r).start()
                return carry
            lax.fori_loop(0, MOE_ROWS, body, 0)

        wait_all_gathers()
        per_half = MOE_STEPS // 2
        for r0 in range(0, MOE_ROWS, 128):
            lo, hi = _unpack_bf16_pairs(hbuf[r0:r0 + 128, :])
            for kk in range(per_half):
                h_scr[kk, r0:r0 + 128, :] = lo[:, kk * MOE_K:(kk + 1) * MOE_K]
                h_scr[per_half + kk, r0:r0 + 128, :] = hi[:, kk * MOE_K:(kk + 1) * MOE_K]

    @pl.when(used & (k == 1))
    def _():
        for r in range(MOE_ROWS):
            gather_row(tok_next_ref[0, r], r).start()

    @pl.when(used)
    def _():
        w1b[...] = w1_ref[...].astype(BF16)
        w3b[...] = w3_ref[...].astype(BF16)
        w2b[pl.ds(pl.multiple_of(k * MOE_F, MOE_F), MOE_F), :] = w2_ref[...].astype(BF16)

        def up(rows):
            h = h_scr[k, rows, :]
            pa = jnp.dot(h, w1b[...], preferred_element_type=F32)
            pb = jnp.dot(h, w3b[...], preferred_element_type=F32)
            a_acc[rows, :] = jnp.where(k == 0, pa, a_acc[rows, :] + pa)
            b_acc[rows, :] = jnp.where(k == 0, pb, b_acc[rows, :] + pb)

        for_pieces(up)

    @pl.when(used & (k == MOE_STEPS - 1))
    def _():
        @pl.when(i > 0)
        def _():
            wait_scatter(nr_ref[jnp.maximum(i - 1, 0)])

        def down(rows):
            a = a_acc[rows, :]
            g = (a * jax.nn.sigmoid(a) * b_acc[rows, :]).astype(BF16)
            y_scr[rows, :] = jnp.dot(g, w2b[...], preferred_element_type=F32)

        for_pieces(down)

        def body(r, carry):
            scatter_row(r, dst_ref[0, r]).start()
            return carry
        lax.fori_loop(0, nr, body, 0)

        @pl.when(i == n_used - 1)
        def _():
            wait_scatter(nr)
            wait_all_gathers()


def moe_ffn(u_packed, slot_tok, slot_dst, blk_e, n_used, n_real, w1, w3, w2):
    n, dh = u_packed.shape
    d = 2 * dh
    n_blk = slot_tok.shape[0]
    assert d == MOE_STEPS * MOE_K and dh % MOE_K == 0

    def w_map(i, k, be, nu, nr):
        return (be[jnp.minimum(i, nu[0] - 1)], jnp.where(i < nu[0], k, MOE_STEPS - 1), 0)

    def slots(shift):
        return pl.BlockSpec((None, 1, MOE_ROWS), lambda i, k, be, nu, nr: (jnp.minimum(i + shift, nu[0] - 1), 0, 0),
                            memory_space=pltpu.SMEM)

    grid_spec = pltpu.PrefetchScalarGridSpec(
        num_scalar_prefetch=3,
        grid=(n_blk, MOE_STEPS),
        in_specs=[slots(0), slots(1), slots(0), pl.BlockSpec(memory_space=pl.ANY),
                  pl.BlockSpec((None, MOE_K, EXPERT_FF), w_map), pl.BlockSpec((None, MOE_K, EXPERT_FF), w_map),
                  pl.BlockSpec((None, MOE_F, d), w_map)],
        out_specs=pl.BlockSpec(memory_space=pl.ANY),
        scratch_shapes=[pltpu.VMEM((MOE_ROWS, dh), jnp.int32),
                        pltpu.VMEM((MOE_STEPS, MOE_ROWS, MOE_K), BF16),
                        pltpu.VMEM((MOE_ROWS, d), F32),
                        pltpu.VMEM((MOE_ROWS, EXPERT_FF), F32), pltpu.VMEM((MOE_ROWS, EXPERT_FF), F32),
                        pltpu.VMEM((MOE_K, EXPERT_FF), BF16), pltpu.VMEM((MOE_K, EXPERT_FF), BF16),
                        pltpu.VMEM((EXPERT_FF, d), BF16),
                        pltpu.SemaphoreType.DMA(()), pltpu.SemaphoreType.DMA(())],
    )
    return pl.pallas_call(
        _moe_kernel,
        out_shape=jax.ShapeDtypeStruct((EXPERT_TOPK * n, d), F32),
        grid_spec=grid_spec,
        compiler_params=_params(("arbitrary", "arbitrary"), 56),
        name="moe_ffn",
    )(blk_e, n_used, n_real, slot_tok, slot_tok, slot_dst, u_packed, w1, w3, w2)


def _combine_kernel(x_ref, y1_ref, y2_ref, w_ref, g_ref, o_ref):
    w = w_ref[...]
    x3 = x_ref[...] + (y1_ref[...] * w[:, 0:1] + y2_ref[...] * w[:, 1:2])
    o_ref[...] = _rms(x3, g_ref[...])


def combine_norm(x, y, w, g, tm=256):
    m, d = x.shape
    row = lambda i: (i, 0)
    return pl.pallas_call(
        _combine_kernel,
        out_shape=jax.ShapeDtypeStruct((m, d), F32),
        grid=(m // tm,),
        in_specs=[pl.BlockSpec((tm, d), row), pl.BlockSpec((None, tm, d), lambda i: (0, i, 0)),
                  pl.BlockSpec((None, tm, d), lambda i: (1, i, 0)),
                  pl.BlockSpec((tm, EXPERT_TOPK), row), pl.BlockSpec((1, d), lambda i: (0, 0))],
        out_specs=pl.BlockSpec((tm, d), row),
        compiler_params=_params(("parallel",), 48),
        name="combine_norm",
    )(x, y, y, w, g.reshape(1, d))


def _rel_bucket(dist):
    max_exact = N_BUCKETS // 2
    d = jnp.maximum(dist, 0)
    df = jnp.maximum(d, 1).astype(F32)
    large = max_exact + (jnp.log(df / max_exact) / math.log(MAX_DISTANCE / max_exact)
                         * (N_BUCKETS - max_exact)).astype(jnp.int32)
    large = jnp.minimum(large, N_BUCKETS - 1)
    return jnp.where(d < max_exact, d, large)


def _swa_bias(bias_table):
    qi = jnp.arange(WINDOW, dtype=jnp.int32)[:, None]
    sj = jnp.arange(2 * WINDOW, dtype=jnp.int32)[None, :]
    dist = qi + WINDOW - sj
    bias = jnp.transpose(bias_table[_rel_bucket(dist)], (2, 0, 1)).astype(F32)
    bias = jnp.where(((dist >= 0) & (dist < WINDOW))[None], bias, -jnp.inf)
    return bias.reshape(A_KV_HEADS, A_GROUP * WINDOW, 2 * WINDOW)


def _dsa_bias(bias_table):
    kk = jnp.arange(Q_BLOCK, dtype=jnp.int32)[:, None]
    qq = jnp.arange(Q_BLOCK, dtype=jnp.int32)[None, :]
    tabs = []
    for dist in (Q_BLOCK + qq - kk, qq - kk):
        b = bias_table[_rel_bucket(dist)].astype(F32) - bias_table[N_BUCKETS - 1].astype(F32)
        tabs.append(jnp.transpose(b, (0, 2, 1)).reshape(Q_BLOCK, B_HEADS * Q_BLOCK))
    return jnp.stack(tabs)


def _moe_routing(rl, b_grp, b_exp):
    n = rl.shape[0]
    grp_logits = rl[:, :N_GROUPS] + b_grp.astype(F32)
    grp_p = jax.nn.softmax(grp_logits, axis=-1)
    g_top = jnp.argmax(grp_logits, axis=-1).astype(jnp.int32)
    g_gate = jnp.take_along_axis(grp_p, g_top[:, None], axis=1)
    exp_logits = (rl[:, N_GROUPS:N_GROUPS + N_EXPERTS] + b_exp.astype(F32)).reshape(n, N_GROUPS, EXPERTS_PER_GROUP)
    in_grp = jnp.take_along_axis(exp_logits, g_top[:, None, None], axis=1)[:, 0]
    top_v, top_j = lax.top_k(in_grp, EXPERT_TOPK)
    gate = g_gate * jax.nn.softmax(top_v, axis=-1)
    eid = (g_top[:, None] * EXPERTS_PER_GROUP + top_j).reshape(-1).astype(jnp.int32)
    n_assign = n * EXPERT_TOPK
    tok_f = jnp.repeat(jnp.arange(n, dtype=jnp.int32), EXPERT_TOPK)
    order = jnp.argsort(eid)
    se = eid[order]
    counts = jnp.bincount(eid, length=N_EXPERTS).astype(jnp.int32)
    padded = ((counts + MOE_ROWS - 1) // MOE_ROWS) * MOE_ROWS
    pad_end = jnp.cumsum(padded)
    pad_start = pad_end - padded
    start = jnp.cumsum(counts) - counts
    dest = pad_start[se] + (jnp.arange(n_assign, dtype=jnp.int32) - start[se])
    n_blk = -(-n_assign // MOE_ROWS) + N_EXPERTS
    n_slots = n_blk * MOE_ROWS
    slot_tok = jnp.zeros((n_slots,), jnp.int32).at[dest].set(tok_f[order])
    slot_dst = jnp.zeros((n_slots,), jnp.int32).at[dest].set((order % EXPERT_TOPK) * n + order // EXPERT_TOPK)
    blk_start = jnp.arange(n_blk, dtype=jnp.int32) * MOE_ROWS
    blk_e = jnp.minimum(jnp.searchsorted(pad_end, blk_start, side='right'), N_EXPERTS - 1).astype(jnp.int32)
    n_real = jnp.clip(counts[blk_e] - (blk_start - pad_start[blk_e]), 0, MOE_ROWS).astype(jnp.int32)
    n_used = (pad_end[-1:] // MOE_ROWS).astype(jnp.int32)
    return (gate, slot_tok.reshape(n_blk, 1, MOE_ROWS), slot_dst.reshape(n_blk, 1, MOE_ROWS), blk_e, n_used, n_real)


def _layer(x, mem, rel_bias, g_mix, w_in, g_cq, w_uq, w_qidx, g_ckv, w_uk, w_uv, g_kidx, sink_a, w_pa, w_pb, w_out,
           g_xattn, g_mem, w_qm, w_km, w_vm, w_om, g_ffn, w_grp, b_grp, w_exp, b_exp, w_e1, w_e3, w_e2):
    bsz, s, d = x.shape
    n = bsz * s
    xt = x.reshape(n, d)
    c_qkv = A_WIDTH + 2 * KV_WIDTH
    c_lat = c_qkv + Q_RANK + KV_RANK + IDX_DIM + IDX_HEADS
    w_qkv = w_in[:, :c_qkv].astype(BF16)
    w_lat = jnp.pad(w_in[:, c_qkv:c_lat], ((0, 0), (0, LATENT_WIDTH - (c_lat - c_qkv)))).astype(BF16)
    w_ga = w_in[:, c_lat:c_lat + d].astype(BF16)
    w_gb = w_in[:, c_lat + d:].astype(BF16)

    u = rmsnorm_cast(xt, g_mix)
    qkv = matmul(u, w_qkv, BF16)
    latent = matmul(u, w_lat, F32)
    o_a = swa_attention(qkv, sink_a.astype(F32), _swa_bias(rel_bias[:, :A_HEADS]), bsz, s)

    ckv, ckvt, kidx = latent_norm(latent, g_ckv, g_kidx, bsz, s)
    qbi = rms_matmul_heads(latent, g_cq, jnp.concatenate([w_uq, w_qidx], axis=1).astype(BF16))
    tri = jnp.tril(jnp.ones((P2_CHUNK, P2_CHUNK), BF16))
    o_b = dsa_attention(qbi, latent, kidx, ckv, ckvt, jnp.transpose(w_uk, (1, 2, 0)).astype(BF16),
                        jnp.transpose(w_uv, (1, 0, 2)).astype(BF16), _dsa_bias(rel_bias[:, A_HEADS:]), tri, bsz, s)

    y = gated_merge(u, o_a, o_b, w_ga, w_gb, w_pa.astype(BF16), w_pb.astype(BF16))
    x1 = matmul_residual(y, w_out.astype(BF16), xt)

    n_mem = mem.shape[1]
    um = rmsnorm_cast(mem.reshape(bsz * n_mem, d), g_mem)
    kv = matmul(um, jnp.concatenate([w_km, w_vm], axis=1).astype(BF16), BF16, tm=bsz * n_mem)
    w_router = jnp.pad(jnp.concatenate([w_grp, w_exp], axis=1),
                       ((0, 0), (0, ROUTER_WIDTH - N_GROUPS - N_EXPERTS))).astype(BF16)
    x2, u3, rl = cross_attention(x1, g_xattn, w_qm.astype(BF16), kv.reshape(bsz, n_mem, 2 * MEM_WIDTH),
                                 w_om.astype(BF16), g_ffn, w_router, bsz, s)

    gate, slot_tok, slot_dst, blk_e, n_used, n_real = _moe_routing(rl, b_grp, b_exp)
    ys = moe_ffn(u3, slot_tok, slot_dst, blk_e, n_used, n_real, w_e1, w_e3, w_e2)
    return x2, ys.reshape(EXPERT_TOPK, n, d), gate


def kernel(x, mem, rel_bias, g_mix, w_in, g_cq, w_uq, w_qidx, g_ckv, w_uk, w_uv, g_kidx, sink_a, w_pa, w_pb, w_out,
           g_xattn, g_mem, w_qm, w_km, w_vm, w_om, g_ffn, w_grp, b_grp, w_exp, b_exp, w_e1, w_e3, w_e2, g_final):
    assert g_mix.shape[0] == 1, "one layer"
    bsz, s, d = x.shape
    x2, ys, gate = _layer(x, mem, rel_bias, g_mix[0], w_in[0], g_cq[0], w_uq[0], w_qidx[0], g_ckv[0], w_uk[0],
                              w_uv[0], g_kidx[0], sink_a[0], w_pa[0], w_pb[0], w_out[0], g_xattn[0], g_mem[0],
                              w_qm[0], w_km[0], w_vm[0], w_om[0], g_ffn[0], w_grp[0], b_grp[0], w_exp[0], b_exp[0],
                              w_e1[0], w_e3[0], w_e2[0])
    out = combine_norm(x2, ys, gate, g_final)
    return out.reshape(bsz, s, d)
```

```python
import functools
import math

import numpy as np
import jax
import jax.numpy as jnp
from jax import lax
from jax.experimental import pallas as pl
from jax.experimental.pallas import tpu as pltpu

F32 = jnp.float32
BF16 = jnp.bfloat16

HEAD_DIM = 128
A_HEADS = 16
A_KV_HEADS = 4
A_GROUP = A_HEADS // A_KV_HEADS
WINDOW = 128
A_WIDTH = A_HEADS * HEAD_DIM
KV_WIDTH = A_KV_HEADS * HEAD_DIM
B_HEADS = 16
B_WIDTH = B_HEADS * HEAD_DIM
Q_RANK = 1024
KV_RANK = 512
IDX_HEADS = 16
IDX_DIM = 128
TOPK_KEYS = 256
Q_BLOCK = 128
N_BUCKETS = 32
MAX_DISTANCE = 128
MEM_HEADS = 4
MEM_WIDTH = MEM_HEADS * HEAD_DIM
N_GROUPS = 8
EXPERTS_PER_GROUP = 8
N_EXPERTS = N_GROUPS * EXPERTS_PER_GROUP
EXPERT_FF = 768
EXPERT_TOPK = 2
EPS = 1e-6

LANE = 128
LATENT_WIDTH = 2048
W_IDX_COL = Q_RANK + KV_RANK + IDX_DIM
MOE_ROWS = 640
MOE_PIECES = ((0, 256), (256, 256), (512, 128))
MOE_STEPS = 4
MOE_K = 1024
MOE_F = EXPERT_FF // MOE_STEPS
assert sum(size for _, size in MOE_PIECES) == MOE_ROWS and EXPERT_FF % MOE_STEPS == 0 and MOE_F % 16 == 0
ROUTER_WIDTH = 128
I16_MIN = -(2 ** 15)
KEY_NEG_INF = int(np.array([0xFF800000 ^ 0x7FFFFFFF], np.uint32).view(np.int32)[0])
NEG_BIG = -1e30
NT_DIMS = (((1,), (1,)), ((), ()))


def _params(semantics, vmem_mb):
    return pltpu.CompilerParams(dimension_semantics=semantics, vmem_limit_bytes=int(vmem_mb * 2 ** 20))


def _rms(xf, g):
    return xf * lax.rsqrt(jnp.mean(xf * xf, axis=-1, keepdims=True) + EPS) * g


def _resident(block, index_map):
    return pl.BlockSpec(block, index_map, pipeline_mode=pl.Buffered(1))


def _rmsnorm_kernel(x_ref, g_ref, o_ref):
    o_ref[...] = _rms(x_ref[...], g_ref[...]).astype(o_ref.dtype)


def rmsnorm_cast(x, g, tm=512):
    m, k = x.shape
    return pl.pallas_call(
        _rmsnorm_kernel,
        out_shape=jax.ShapeDtypeStruct((m, k), BF16),
        grid=(m // tm,),
        in_specs=[pl.BlockSpec((tm, k), lambda i: (i, 0)), pl.BlockSpec((1, k), lambda i: (0, 0))],
        out_specs=pl.BlockSpec((tm, k), lambda i: (i, 0)),
        compiler_params=_params(("parallel",), 40),
        name="rmsnorm_cast",
    )(x, g.reshape(1, k))


def _mm_kernel(a_ref, w_ref, o_ref):
    o_ref[...] = jnp.dot(a_ref[...], w_ref[...], preferred_element_type=F32).astype(o_ref.dtype)


def matmul(a, w, out_dtype, tm=1024, tn=512):
    m, k = a.shape
    n = w.shape[1]
    return pl.pallas_call(
        _mm_kernel,
        out_shape=jax.ShapeDtypeStruct((m, n), out_dtype),
        grid=(m // tm, n // tn),
        in_specs=[pl.BlockSpec((tm, k), lambda i, j: (i, 0)), pl.BlockSpec((k, tn), lambda i, j: (0, j))],
        out_specs=pl.BlockSpec((tm, tn), lambda i, j: (i, j)),
        compiler_params=_params(("parallel", "arbitrary"), 48),
        name="matmul",
    )(a, w)


def _mm_res_kernel(a_ref, w_ref, r_ref, o_ref):
    o_ref[...] = r_ref[...] + jnp.dot(a_ref[...], w_ref[...], preferred_element_type=F32)


def matmul_residual(a, w, res, tm=1024, tn=512):
    m, k = a.shape
    n = w.shape[1]
    return pl.pallas_call(
        _mm_res_kernel,
        out_shape=jax.ShapeDtypeStruct((m, n), F32),
        grid=(m // tm, n // tn),
        in_specs=[pl.BlockSpec((tm, k), lambda i, j: (i, 0)), pl.BlockSpec((k, tn), lambda i, j: (0, j)),
                  pl.BlockSpec((tm, tn), lambda i, j: (i, j))],
        out_specs=pl.BlockSpec((tm, tn), lambda i, j: (i, j)),
        compiler_params=_params(("parallel", "arbitrary"), 48),
        name="matmul_residual",
    )(a, w, res)


def _rms_mm_heads_kernel(x_ref, g_ref, w_ref, o_ref, u_ref):
    @pl.when(pl.program_id(1) == 0)
    def _():
        u_ref[...] = _rms(x_ref[...], g_ref[...]).astype(BF16)

    r = jnp.dot(u_ref[...], w_ref[...], preferred_element_type=F32)
    for hh in range(o_ref.shape[0]):
        o_ref[hh] = r[:, hh * LANE:(hh + 1) * LANE].astype(o_ref.dtype)


def rms_matmul_heads(x, g, w, tm=1024, tn=512):
    m = x.shape[0]
    k, n = w.shape
    hb = tn // LANE
    return pl.pallas_call(
        _rms_mm_heads_kernel,
        out_shape=jax.ShapeDtypeStruct((n // LANE, m, LANE), BF16),
        grid=(m // tm, n // tn),
        in_specs=[pl.BlockSpec((tm, k), lambda i, j: (i, 0)), pl.BlockSpec((1, k), lambda i, j: (0, 0)),
                  pl.BlockSpec((k, tn), lambda i, j: (0, j))],
        out_specs=pl.BlockSpec((hb, tm, LANE), lambda i, j: (j, i, 0)),
        scratch_shapes=[pltpu.VMEM((tm, k), BF16)],
        compiler_params=_params(("parallel", "arbitrary"), 40),
        name="rms_matmul_heads",
    )(x, g.reshape(1, k), w)


def _swa_kernel(sink_ref, q_ref, kp_ref, kc_ref, vp_ref, vc_ref, bias_ref, o_ref):
    n = pl.program_id(1)
    kb = jnp.concatenate([kp_ref[...], kc_ref[...]], axis=0)
    vb = jnp.concatenate([vp_ref[...], vc_ref[...]], axis=0)
    col = lax.broadcasted_iota(jnp.int32, (1, 2 * WINDOW), 1)
    no_prev = jnp.where((col < WINDOW) & (n == 0), -jnp.inf, 0.0)
    for kh in range(A_KV_HEADS):
        heads = [kh * A_GROUP + g for g in range(A_GROUP)]
        qs = jnp.concatenate([q_ref[:, h * HEAD_DIM:(h + 1) * HEAD_DIM] for h in heads], axis=0)
        k = kb[:, kh * HEAD_DIM:(kh + 1) * HEAD_DIM]
        v = vb[:, kh * HEAD_DIM:(kh + 1) * HEAD_DIM]
        lg = lax.dot_general(qs, k, NT_DIMS, preferred_element_type=F32) * (HEAD_DIM ** -0.5)
        lg = lg + bias_ref[kh] + no_prev
        sk = jnp.concatenate([jnp.full((WINDOW, 1), sink_ref[h], F32) for h in heads], axis=0)
        mx = jnp.maximum(jnp.max(lg, axis=-1, keepdims=True), sk)
        p = jnp.exp(lg - mx)
        den = jnp.sum(p, axis=-1, keepdims=True) + jnp.exp(sk - mx)
        o = jnp.dot((p * (1.0 / den)).astype(BF16), v, preferred_element_type=F32)
        for g, h in enumerate(heads):
            o_ref[:, h * HEAD_DIM:(h + 1) * HEAD_DIM] = o[g * WINDOW:(g + 1) * WINDOW].astype(o_ref.dtype)


def swa_attention(qkv, sink, bias, bsz, s):
    nb = s // WINDOW
    kcol = A_WIDTH // KV_WIDTH
    cur = lambda b, n: b * nb + n
    prev = lambda b, n: b * nb + jnp.maximum(n - 1, 0)
    return pl.pallas_call(
        _swa_kernel,
        out_shape=jax.ShapeDtypeStruct((bsz * s, A_WIDTH), BF16),
        grid=(bsz, nb),
        in_specs=[
            pl.BlockSpec(memory_space=pltpu.SMEM),
            pl.BlockSpec((WINDOW, A_WIDTH), lambda b, n: (cur(b, n), 0)),
            pl.BlockSpec((WINDOW, KV_WIDTH), lambda b, n: (prev(b, n), kcol)),
            pl.BlockSpec((WINDOW, KV_WIDTH), lambda b, n: (cur(b, n), kcol)),
            pl.BlockSpec((WINDOW, KV_WIDTH), lambda b, n: (prev(b, n), kcol + 1)),
            pl.BlockSpec((WINDOW, KV_WIDTH), lambda b, n: (cur(b, n), kcol + 1)),
            pl.BlockSpec((A_KV_HEADS, A_GROUP * WINDOW, 2 * WINDOW), lambda b, n: (0, 0, 0)),
        ],
        out_specs=pl.BlockSpec((WINDOW, A_WIDTH), lambda b, n: (cur(b, n), 0)),
        compiler_params=_params(("parallel", "arbitrary"), 32),
        name="swa_attention",
    )(sink, qkv, qkv, qkv, qkv, qkv, bias)


def _latent_norm_kernel(x_ref, gkv_ref, gk_ref, ckv_ref, ckvt_ref, kidx_ref):
    c = _rms(x_ref[:, :KV_RANK], gkv_ref[...])
    ckv_ref[...] = c.astype(BF16)
    ckvt_ref[...] = c.T.astype(BF16)
    kidx_ref[...] = _rms(x_ref[:, KV_RANK:KV_RANK + IDX_DIM], gk_ref[...]).astype(BF16)


def latent_norm(latent, g_ckv, g_kidx, bsz, s, tm=512):
    per = s // tm
    half = LATENT_WIDTH // 2
    return pl.pallas_call(
        _latent_norm_kernel,
        out_shape=(jax.ShapeDtypeStruct((bsz, s, KV_RANK), BF16), jax.ShapeDtypeStruct((bsz, KV_RANK, s), BF16),
                   jax.ShapeDtypeStruct((bsz, s, IDX_DIM), BF16)),
        grid=(bsz * per,),
        in_specs=[pl.BlockSpec((tm, half), lambda i: (i, 1)), pl.BlockSpec((1, KV_RANK), lambda i: (0, 0)),
                  pl.BlockSpec((1, IDX_DIM), lambda i: (0, 0))],
        out_specs=(pl.BlockSpec((None, tm, KV_RANK), lambda i: (i // per, i % per, 0)),
                   pl.BlockSpec((None, KV_RANK, tm), lambda i: (i // per, 0, i % per)),
                   pl.BlockSpec((None, tm, IDX_DIM), lambda i: (i // per, i % per, 0))),
        compiler_params=_params(("parallel",), 32),
        name="latent_norm",
    )(latent, g_ckv.reshape(1, KV_RANK), g_kidx.reshape(1, IDX_DIM))


P1_CHUNK = 1024
P2_CHUNK = 256
HEAD_PAIR = 2


def _sortable_key(x):
    bits = lax.bitcast_convert_type(x, jnp.int32)
    return bits ^ ((bits >> 31) & 0x7FFFFFFF)


def _dsa_kernel(qb_ref, qi_ref, w_ref, kidx_ref, ckv_ref, ckvt_ref, wuk_ref, wuv_ref, tab_ref, tri_ref, o_ref,
                key_ref, hi_ref, lo_ref, qlat_ref, acc_ref, m_ref, l_ref):
    n = pl.program_id(1)
    nheads = B_HEADS
    lane_q = lax.broadcasted_iota(jnp.int32, (1, Q_BLOCK), 1)
    t_row = n * Q_BLOCK + lane_q

    for h in range(nheads):
        ql = jnp.dot(qb_ref[h], wuk_ref[h], preferred_element_type=F32) * (HEAD_DIM ** -0.5)
        qlat_ref[h * Q_BLOCK:(h + 1) * Q_BLOCK, :] = ql.astype(BF16)

    w_t = w_ref[...].T

    n_p1 = (n * Q_BLOCK + Q_BLOCK + P1_CHUNK - 1) // P1_CHUNK

    def p1_body(c, carry):
        start = pl.multiple_of(c * P1_CHUNK, P1_CHUNK)
        kc = kidx_ref[pl.ds(start, P1_CHUNK), :]
        score = jnp.zeros((P1_CHUNK, Q_BLOCK), F32)
        hg = 4
        for g0 in range(0, IDX_HEADS, hg):
            q4 = qi_ref[g0:g0 + hg].reshape(hg * Q_BLOCK, IDX_DIM)
            st = lax.dot_general(kc, q4, NT_DIMS, preferred_element_type=F32)
            for j in range(hg):
                h = g0 + j
                score = score + jnp.maximum(st[:, j * Q_BLOCK:(j + 1) * Q_BLOCK], 0.0) * w_t[h:h + 1, :]
        score = score * (IDX_DIM ** -0.5 * IDX_HEADS ** -0.5)
        s_idx = start + lax.broadcasted_iota(jnp.int32, (P1_CHUNK, Q_BLOCK), 0)
        score = jnp.where(s_idx <= t_row, score, -jnp.inf)
        key = _sortable_key(score)
        key_ref[pl.ds(start, P1_CHUNK), :] = key
        hi_ref[pl.ds(start, P1_CHUNK), :] = (key >> 16).astype(jnp.int16)
        lo_ref[pl.ds(start, P1_CHUNK), :] = ((key & 0xFFFF) + I16_MIN).astype(jnp.int16)
        return carry

    lax.fori_loop(0, n_p1, p1_body, 0)

    def as_i16(x):
        return x.astype(jnp.int16)

    def for_planes(body, init):
        sub = 512

        def chunk_body(c, carry):
            for j in range(P1_CHUNK // sub):
                carry = body(pl.ds(pl.multiple_of(c * P1_CHUNK + j * sub, sub), sub), carry)
            return carry

        return lax.fori_loop(0, n_p1, chunk_body, init)

    def count16(plane_ref, pred_fn):
        rows = 16

        def body(sl, acc):
            blk = plane_ref[sl, :]
            hit = jnp.where(pred_fn(blk), jnp.int16(1), jnp.int16(0))
            parts = [hit[t * rows:(t + 1) * rows] for t in range(hit.shape[0] // rows)]
            while len(parts) > 1:
                parts = [a + b for a, b in zip(parts[::2], parts[1::2])]
            return acc + parts[0].astype(jnp.int32)

        acc = for_planes(body, jnp.zeros((rows, Q_BLOCK), jnp.int32))
        return acc.sum(axis=0, keepdims=True)

    def kth_largest16(plane_ref, kth):
        def bit_body(i, thr):
            cand = thr + jnp.left_shift(jnp.int32(1), 15 - i)
            cand16 = as_i16(cand)
            return jnp.where(count16(plane_ref, lambda blk: blk >= cand16) >= kth, cand, thr)

        return lax.fori_loop(0, 16, bit_body, jnp.full((1, Q_BLOCK), I16_MIN, jnp.int32))

    thr_hi = kth_largest16(hi_ref, TOPK_KEYS)
    thr_hi16 = as_i16(thr_hi)
    still = TOPK_KEYS - count16(hi_ref, lambda blk: blk > thr_hi16)

    def mask_lo(sl, carry):
        lo_ref[sl, :] = jnp.where(hi_ref[sl, :] == thr_hi16, lo_ref[sl, :], jnp.int16(I16_MIN))
        return carry

    for_planes(mask_lo, 0)
    thr_lo = kth_largest16(lo_ref, still)
    thr_lo16 = as_i16(thr_lo)
    thr = thr_hi * 65536 + (thr_lo - I16_MIN)
    need = (still - count16(lo_ref, lambda blk: blk > thr_lo16)).astype(F32)

    m_ref[...] = jnp.full(m_ref.shape, NEG_BIG, F32)
    l_ref[...] = jnp.zeros(l_ref.shape, F32)
    acc_ref[...] = jnp.zeros(acc_ref.shape, F32)

    def chunk(c, eq_seen, near):
        start = pl.multiple_of(c * P2_CHUNK, P2_CHUNK)
        u = key_ref[pl.ds(start, P2_CHUNK), :]
        eq = u == thr
        pref = jnp.dot(tri_ref[...], jnp.where(eq, 1.0, 0.0).astype(BF16), preferred_element_type=F32)
        sel = ((u > thr) | (eq & (eq_seen + pref <= need))) & (u != KEY_NEG_INF)
        mask = jnp.where(sel, 0.0, -jnp.inf)
        eq_seen = eq_seen + pref[P2_CHUNK - 1:P2_CHUNK, :]
        kc = ckv_ref[pl.ds(start, P2_CHUNK), :]
        kct = ckvt_ref[:, pl.ds(start, P2_CHUNK)]
        if near:
            halves = []
            for half in range(P2_CHUNK // Q_BLOCK):
                blk = c * (P2_CHUNK // Q_BLOCK) + half
                halves.append(jnp.where(blk == n, tab_ref[1], jnp.where(blk == n - 1, tab_ref[0], 0.0)))
        def pair_logits(hp):
            rows = slice(hp * HEAD_PAIR * Q_BLOCK, (hp + 1) * HEAD_PAIR * Q_BLOCK)
            return lax.dot_general(kc, qlat_ref[rows, :], NT_DIMS, preferred_element_type=F32)

        n_pairs = nheads // HEAD_PAIR
        lg_next = pair_logits(0)
        for hp in range(n_pairs):
            lg = lg_next
            if hp + 1 < n_pairs:
                lg_next = pair_logits(hp + 1)
            ps, alphas = [], []
            for j in range(HEAD_PAIR):
                h = hp * HEAD_PAIR + j
                hs = slice(h * Q_BLOCK, (h + 1) * Q_BLOCK)
                lh = lg[:, j * Q_BLOCK:(j + 1) * Q_BLOCK] + mask
                if near:
                    lh = lh + jnp.concatenate([b[:, hs] for b in halves], axis=0)
                m_old = m_ref[:, hs]
                m_new = jnp.maximum(m_old, jnp.max(lh, axis=0, keepdims=True))
                alpha = jnp.exp(m_old - m_new)
                p = jnp.exp(lh - m_new)
                l_ref[:, hs] = alpha * l_ref[:, hs] + jnp.sum(p, axis=0, keepdims=True)
                m_ref[:, hs] = m_new
                ps.append(p.astype(BF16))
                alphas.append(alpha)
            pv = jnp.dot(kct, jnp.concatenate(ps, axis=1), preferred_element_type=F32)
            acc_ref[hp] = acc_ref[hp] * jnp.concatenate(alphas, axis=1) + pv
        return eq_seen

    per = P2_CHUNK // Q_BLOCK
    c_end = n // per + 1
    c_near = jnp.maximum(n - 1, 0) // per
    eq_seen = lax.fori_loop(0, c_near, functools.partial(chunk, near=False), jnp.zeros((1, Q_BLOCK), F32))
    lax.fori_loop(c_near, c_end, functools.partial(chunk, near=True), eq_seen)

    for h in range(nheads):
        hs = slice(h * Q_BLOCK, (h + 1) * Q_BLOCK)
        in_pair = slice((h % HEAD_PAIR) * Q_BLOCK, (h % HEAD_PAIR + 1) * Q_BLOCK)
        o_lat = (acc_ref[h // HEAD_PAIR, :, in_pair] * (1.0 / l_ref[:, hs])).T
        o = jnp.dot(o_lat.astype(BF16), wuv_ref[h], preferred_element_type=F32)
        o_ref[:, h * HEAD_DIM:(h + 1) * HEAD_DIM] = o.astype(o_ref.dtype)


def dsa_attention(qbi, latent, kidx, ckv, ckvt, wuk_t, wuv_h, tab, tri, bsz, s):
    nb = s // Q_BLOCK
    row = lambda b, n: b * nb + n
    hq = B_HEADS * Q_BLOCK
    return pl.pallas_call(
        _dsa_kernel,
        out_shape=jax.ShapeDtypeStruct((bsz * s, B_WIDTH), BF16),
        grid=(bsz, nb),
        in_specs=[
            pl.BlockSpec((B_HEADS, Q_BLOCK, HEAD_DIM), lambda b, n: (0, row(b, n), 0)),
            pl.BlockSpec((IDX_HEADS, Q_BLOCK, IDX_DIM), lambda b, n: (1, row(b, n), 0)),
            pl.BlockSpec((Q_BLOCK, LANE), lambda b, n: (row(b, n), W_IDX_COL // LANE)),
            _resident((None, s, IDX_DIM), lambda b, n: (b, 0, 0)),
            _resident((None, s, KV_RANK), lambda b, n: (b, 0, 0)),
            _resident((None, KV_RANK, s), lambda b, n: (b, 0, 0)),
            _resident((B_HEADS, HEAD_DIM, KV_RANK), lambda b, n: (0, 0, 0)),
            _resident((B_HEADS, KV_RANK, HEAD_DIM), lambda b, n: (0, 0, 0)),
            _resident((2, Q_BLOCK, hq), lambda b, n: (0, 0, 0)),
            _resident((P2_CHUNK, P2_CHUNK), lambda b, n: (0, 0)),
        ],
        out_specs=pl.BlockSpec((Q_BLOCK, B_WIDTH), lambda b, n: (row(b, n), 0)),
        scratch_shapes=[
            pltpu.VMEM((s, Q_BLOCK), jnp.int32),
            pltpu.VMEM((s, Q_BLOCK), jnp.int16),
            pltpu.VMEM((s, Q_BLOCK), jnp.int16),
            pltpu.VMEM((hq, KV_RANK), BF16),
            pltpu.VMEM((B_HEADS // HEAD_PAIR, KV_RANK, HEAD_PAIR * Q_BLOCK), F32),
            pltpu.VMEM((1, hq), F32),
            pltpu.VMEM((1, hq), F32),
        ],
        compiler_params=_params(("arbitrary", "arbitrary"), 56),
        name="dsa_attention",
    )(qbi, qbi, latent, kidx, ckv, ckvt, wuk_t, wuv_h, tab, tri)


def _merge_kernel(u_ref, oa_ref, ob_ref, wga_ref, wgb_ref, wpa_ref, wpb_ref, y_ref):
    u = u_ref[...]
    ga = jnp.dot(u, wga_ref[...], preferred_element_type=F32)
    gb = jnp.dot(u, wgb_ref[...], preferred_element_type=F32)
    pa = jnp.dot(oa_ref[...], wpa_ref[...], preferred_element_type=F32)
    pb = jnp.dot(ob_ref[...], wpb_ref[...], preferred_element_type=F32)
    y_ref[...] = (jax.nn.sigmoid(ga) * pa + jax.nn.sigmoid(gb) * pb).astype(y_ref.dtype)


def gated_merge(u, o_a, o_b, w_ga, w_gb, w_pa, w_pb, tm=512, tn=512):
    m, d = u.shape
    row = lambda i, j: (i, 0)
    col = lambda i, j: (0, j)
    return pl.pallas_call(
        _merge_kernel,
        out_shape=jax.ShapeDtypeStruct((m, d), BF16),
        grid=(m // tm, d // tn),
        in_specs=[pl.BlockSpec((tm, d), row), pl.BlockSpec((tm, A_WIDTH), row), pl.BlockSpec((tm, B_WIDTH), row),
                  pl.BlockSpec((d, tn), col), pl.BlockSpec((d, tn), col),
                  pl.BlockSpec((A_WIDTH, tn), col), pl.BlockSpec((B_WIDTH, tn), col)],
        out_specs=pl.BlockSpec((tm, tn), lambda i, j: (i, j)),
        compiler_params=_params(("parallel", "arbitrary"), 56),
        name="gated_merge",
    )(u, o_a, o_b, w_ga, w_gb, w_pa, w_pb)


def _xattn_kernel(x_ref, gx_ref, wq_ref, kv_ref, wo_ref, gf_ref, wr_ref, x2_ref, u3_ref, rl_ref):
    x = x_ref[...]
    u = _rms(x, gx_ref[...]).astype(BF16)
    q = jnp.dot(u, wq_ref[...], preferred_element_type=F32).astype(BF16)
    outs = []
    for h in range(MEM_HEADS):
        hs = slice(h * HEAD_DIM, (h + 1) * HEAD_DIM)
        lg = lax.dot_general(q[:, hs], kv_ref[:, hs], NT_DIMS, preferred_element_type=F32) * (HEAD_DIM ** -0.5)
        p = jnp.exp(lg - jnp.max(lg, axis=-1, keepdims=True))
        p = p * (1.0 / jnp.sum(p, axis=-1, keepdims=True))
        v = kv_ref[:, MEM_WIDTH + h * HEAD_DIM:MEM_WIDTH + (h + 1) * HEAD_DIM]
        outs.append(jnp.dot(p.astype(BF16), v, preferred_element_type=F32).astype(BF16))
    o = jnp.concatenate(outs, axis=1)
    x2 = x + jnp.dot(o, wo_ref[...], preferred_element_type=F32)
    x2_ref[...] = x2
    u3 = _rms(x2, gf_ref[...]).astype(BF16)
    u3_ref[...] = _pack_bf16_pairs(u3)
    rl_ref[...] = jnp.dot(u3, wr_ref[...], preferred_element_type=F32)


def _pack_bf16_pairs(a):
    k = a.shape[1] // 2
    lo = lax.bitcast_convert_type(a[:, :k].astype(F32), jnp.int32)
    hi = lax.bitcast_convert_type(a[:, k:].astype(F32), jnp.int32)
    return ((lo >> 16) & 0xFFFF) | (hi & jnp.int32(-65536))


def _unpack_bf16_pairs(w):
    lo = lax.bitcast_convert_type(w << 16, F32).astype(BF16)
    hi = lax.bitcast_convert_type(w & jnp.int32(-65536), F32).astype(BF16)
    return lo, hi


def cross_attention(x, g_x, w_qm, kv, w_om, g_ffn, w_router, bsz, s, tm=256):
    m, d = x.shape
    per = s // tm
    n_mem = kv.shape[1]
    const = lambda i: (0, 0)
    return pl.pallas_call(
        _xattn_kernel,
        out_shape=(jax.ShapeDtypeStruct((m, d), F32), jax.ShapeDtypeStruct((m, d // 2), jnp.int32),
                   jax.ShapeDtypeStruct((m, ROUTER_WIDTH), F32)),
        grid=(m // tm,),
        in_specs=[pl.BlockSpec((tm, d), lambda i: (i, 0)), pl.BlockSpec((1, d), const),
                  _resident((d, MEM_WIDTH), const),
                  pl.BlockSpec((None, n_mem, 2 * MEM_WIDTH), lambda i: (i // per, 0, 0)),
                  _resident((MEM_WIDTH, d), const), pl.BlockSpec((1, d), const),
                  _resident((d, ROUTER_WIDTH), const)],
        out_specs=(pl.BlockSpec((tm, d), lambda i: (i, 0)), pl.BlockSpec((tm, d // 2), lambda i: (i, 0)),
                   pl.BlockSpec((tm, ROUTER_WIDTH), lambda i: (i, 0))),
        compiler_params=_params(("parallel",), 56),
        name="cross_attention",
    )(x, g_x.reshape(1, d), w_qm, kv, w_om, g_ffn.reshape(1, d), w_router)


def _moe_kernel(be_ref, nu_ref, nr_ref, tok_ref, tok_next_ref, dst_ref, u_hbm, w1_ref, w3_ref, w2_ref, y_hbm,
                hbuf, h_scr, y_scr, a_acc, b_acc, w1b, w3b, w2b, gsem, ssem):
    i = pl.program_id(0)
    k = pl.program_id(1)
    n_used = nu_ref[0]
    used = i < n_used
    nr = nr_ref[i]

    def gather_row(tok, r):
        return pltpu.make_async_copy(u_hbm.at[pl.ds(tok, 1), :], hbuf.at[pl.ds(r, 1), :], gsem)

    def scatter_row(r, dst):
        return pltpu.make_async_copy(y_scr.at[pl.ds(r, 1), :], y_hbm.at[pl.ds(dst, 1), :], ssem)

    def wait_all_gathers():
        def body(r, carry):
            gather_row(0, r).wait()
            return carry
        lax.fori_loop(0, MOE_ROWS, body, 0, unroll=8)

    def wait_scatter(count):
        def body(r, carry):
            scatter_row(r, 0).wait()
            return carry
        lax.fori_loop(0, count, body, 0)

    def for_pieces(fn):
        for p, (r0, size) in enumerate(MOE_PIECES):
            if p == 0:
                fn(slice(r0, r0 + size))
            else:
                pl.when(nr > r0)(functools.partial(fn, slice(r0, r0 + size)))

    @pl.when(used & (k == 0))
    def _():
        @pl.when(i == 0)
        def _():
            a_acc[...] = jnp.zeros(a_acc.shape, a_acc.dtype)
            b_acc[...] = jnp.zeros(b_acc.shape, b_acc.dtype)

            def body(r, carry):
                gather_row(tok_ref[0, r], r).start()
                return carry
            lax.fori_loop(0, MOE_ROWS, body, 0)

        wait_all_gathers()
        per_half = MOE_STEPS // 2
        for r0 in range(0, MOE_ROWS, 128):
            lo, hi = _unpack_bf16_pairs(hbuf[r0:r0 + 128, :])
            for kk in range(per_half):
                h_scr[kk, r0:r0 + 128, :] = lo[:, kk * MOE_K:(kk + 1) * MOE_K]
                h_scr[per_half + kk, r0:r0 + 128, :] = hi[:, kk * MOE_K:(kk + 1) * MOE_K]

    @pl.when(used & (k == 1))
    def _():
        for r in range(MOE_ROWS):
            gather_row(tok_next_ref[0, r], r).start()

    @pl.when(used)
    def _():
        w1b[...] = w1_ref[...].astype(BF16)
        w3b[...] = w3_ref[...].astype(BF16)
        w2b[pl.ds(pl.multiple_of(k * MOE_F, MOE_F), MOE_F), :] = w2_ref[...].astype(BF16)

        def up(rows):
            h = h_scr[k, rows, :]
            pa = jnp.dot(h, w1b[...], preferred_element_type=F32)
            pb = jnp.dot(h, w3b[...], preferred_element_type=F32)
            a_acc[rows, :] = jnp.where(k == 0, pa, a_acc[rows, :] + pa)
            b_acc[rows, :] = jnp.where(k == 0, pb, b_acc[rows, :] + pb)

        for_pieces(up)

    @pl.when(used & (k == MOE_STEPS - 1))
    def _():
        @pl.when(i > 0)
        def _():
            wait_scatter(nr_ref[jnp.maximum(i - 1, 0)])

        def down(rows):
            a = a_acc[rows, :]
            g = (a * jax.nn.sigmoid(a) * b_acc[rows, :]).astype(BF16)
            y_scr[rows, :] = jnp.dot(g, w2b[...], preferred_element_type=F32)

        for_pieces(down)

        def body(r, carry):
            scatter_row(r, dst_ref[0, r]).start()
            return carry
        lax.fori_loop(0, nr, body, 0)

        @pl.when(i == n_used - 1)
        def _():
            wait_scatter(nr)
            wait_all_gathers()


def moe_ffn(u_packed, slot_tok, slot_dst, blk_e, n_used, n_real, w1, w3, w2):
    n, dh = u_packed.shape
    d = 2 * dh
    n_blk = slot_tok.shape[0]
    assert d == MOE_STEPS * MOE_K and dh % MOE_K == 0

    def w_map(i, k, be, nu, nr):
        return (be[jnp.minimum(i, nu[0] - 1)], jnp.where(i < nu[0], k, MOE_STEPS - 1), 0)

    def slots(shift):
        return pl.BlockSpec((None, 1, MOE_ROWS), lambda i, k, be, nu, nr: (jnp.minimum(i + shift, nu[0] - 1), 0, 0),
                            memory_space=pltpu.SMEM)

    grid_spec = pltpu.PrefetchScalarGridSpec(
        num_scalar_prefetch=3,
        grid=(n_blk, MOE_STEPS),
        in_specs=[slots(0), slots(1), slots(0), pl.BlockSpec(memory_space=pl.ANY),
                  pl.BlockSpec((None, MOE_K, EXPERT_FF), w_map), pl.BlockSpec((None, MOE_K, EXPERT_FF), w_map),
                  pl.BlockSpec((None, MOE_F, d), w_map)],
        out_specs=pl.BlockSpec(memory_space=pl.ANY),
        scratch_shapes=[pltpu.VMEM((MOE_ROWS, dh), jnp.int32),
                        pltpu.VMEM((MOE_STEPS, MOE_ROWS, MOE_K), BF16),
                        pltpu.VMEM((MOE_ROWS, d), F32),
                        pltpu.VMEM((MOE_ROWS, EXPERT_FF), F32), pltpu.VMEM((MOE_ROWS, EXPERT_FF), F32),
                        pltpu.VMEM((MOE_K, EXPERT_FF), BF16), pltpu.VMEM((MOE_K, EXPERT_FF), BF16),
                        pltpu.VMEM((EXPERT_FF, d), BF16),
                        pltpu.SemaphoreType.DMA(()), pltpu.SemaphoreType.DMA(())],
    )
    return pl.pallas_call(
        _moe_kernel,
        out_shape=jax.ShapeDtypeStruct((EXPERT_TOPK * n, d), F32),
        grid_spec=grid_spec,
        compiler_params=_params(("arbitrary", "arbitrary"), 56),
        name="moe_ffn",
    )(blk_e, n_used, n_real, slot_tok, slot_tok, slot_dst, u_packed, w1, w3, w2)


def _combine_kernel(x_ref, y1_ref, y2_ref, w_ref, g_ref, o_ref):
    w = w_ref[...]
    x3 = x_ref[...] + (y1_ref[...] * w[:, 0:1] + y2_ref[...] * w[:, 1:2])
    o_ref[...] = _rms(x3, g_ref[...])


def combine_norm(x, y, w, g, tm=256):
    m, d = x.shape
    row = lambda i: (i, 0)
    return pl.pallas_call(
        _combine_kernel,
        out_shape=jax.ShapeDtypeStruct((m, d), F32),
        grid=(m // tm,),
        in_specs=[pl.BlockSpec((tm, d), row), pl.BlockSpec((None, tm, d), lambda i: (0, i, 0)),
                  pl.BlockSpec((None, tm, d), lambda i: (1, i, 0)),
                  pl.BlockSpec((tm, EXPERT_TOPK), row), pl.BlockSpec((1, d), lambda i: (0, 0))],
        out_specs=pl.BlockSpec((tm, d), row),
        compiler_params=_params(("parallel",), 48),
        name="combine_norm",
    )(x, y, y, w, g.reshape(1, d))


def _rel_bucket(dist):
    max_exact = N_BUCKETS // 2
    d = jnp.maximum(dist, 0)
    df = jnp.maximum(d, 1).astype(F32)
    large = max_exact + (jnp.log(df / max_exact) / math.log(MAX_DISTANCE / max_exact)
                         * (N_BUCKETS - max_exact)).astype(jnp.int32)
    large = jnp.minimum(large, N_BUCKETS - 1)
    return jnp.where(d < max_exact, d, large)


def _swa_bias(bias_table):
    qi = jnp.arange(WINDOW, dtype=jnp.int32)[:, None]
    sj = jnp.arange(2 * WINDOW, dtype=jnp.int32)[None, :]
    dist = qi + WINDOW - sj
    bias = jnp.transpose(bias_table[_rel_bucket(dist)], (2, 0, 1)).astype(F32)
    bias = jnp.where(((dist >= 0) & (dist < WINDOW))[None], bias, -jnp.inf)
    return bias.reshape(A_KV_HEADS, A_GROUP * WINDOW, 2 * WINDOW)


def _dsa_bias(bias_table):
    kk = jnp.arange(Q_BLOCK, dtype=jnp.int32)[:, None]
    qq = jnp.arange(Q_BLOCK, dtype=jnp.int32)[None, :]
    tabs = []
    for dist in (Q_BLOCK + qq - kk, qq - kk):
        b = bias_table[_rel_bucket(dist)].astype(F32) - bias_table[N_BUCKETS - 1].astype(F32)
        tabs.append(jnp.transpose(b, (0, 2, 1)).reshape(Q_BLOCK, B_HEADS * Q_BLOCK))
    return jnp.stack(tabs)


def _moe_routing(rl, b_grp, b_exp):
    n = rl.shape[0]
    grp_logits = rl[:, :N_GROUPS] + b_grp.astype(F32)
    grp_p = jax.nn.softmax(grp_logits, axis=-1)
    g_top = jnp.argmax(grp_logits, axis=-1).astype(jnp.int32)
    g_gate = jnp.take_along_axis(grp_p, g_top[:, None], axis=1)
    exp_logits = (rl[:, N_GROUPS:N_GROUPS + N_EXPERTS] + b_exp.astype(F32)).reshape(n, N_GROUPS, EXPERTS_PER_GROUP)
    in_grp = jnp.take_along_axis(exp_logits, g_top[:, None, None], axis=1)[:, 0]
    top_v, top_j = lax.top_k(in_grp, EXPERT_TOPK)
    gate = g_gate * jax.nn.softmax(top_v, axis=-1)
    eid = (g_top[:, None] * EXPERTS_PER_GROUP + top_j).reshape(-1).astype(jnp.int32)
    n_assign = n * EXPERT_TOPK
    tok_f = jnp.repeat(jnp.arange(n, dtype=jnp.int32), EXPERT_TOPK)
    order = jnp.argsort(eid)
    se = eid[order]
    counts = jnp.bincount(eid, length=N_EXPERTS).astype(jnp.int32)
    padded = ((counts + MOE_ROWS - 1) // MOE_ROWS) * MOE_ROWS
    pad_end = jnp.cumsum(padded)
    pad_start = pad_end - padded
    start = jnp.cumsum(counts) - counts
    dest = pad_start[se] + (jnp.arange(n_assign, dtype=jnp.int32) - start[se])
    n_blk = -(-n_assign // MOE_ROWS) + N_EXPERTS
    n_slots = n_blk * MOE_ROWS
    slot_tok = jnp.zeros((n_slots,), jnp.int32).at[dest].set(tok_f[order])
    slot_dst = jnp.zeros((n_slots,), jnp.int32).at[dest].set((order % EXPERT_TOPK) * n + order // EXPERT_TOPK)
    blk_start = jnp.arange(n_blk, dtype=jnp.int32) * MOE_ROWS
    blk_e = jnp.minimum(jnp.searchsorted(pad_end, blk_start, side='right'), N_EXPERTS - 1).astype(jnp.int32)
    n_real = jnp.clip(counts[blk_e] - (blk_start - pad_start[blk_e]), 0, MOE_ROWS).astype(jnp.int32)
    n_used = (pad_end[-1:] // MOE_ROWS).astype(jnp.int32)
    return (gate, slot_tok.reshape(n_blk, 1, MOE_ROWS), slot_dst.reshape(n_blk, 1, MOE_ROWS), blk_e, n_used, n_real)


def _layer(x, mem, rel_bias, g_mix, w_in, g_cq, w_uq, w_qidx, g_ckv, w_uk, w_uv, g_kidx, sink_a, w_pa, w_pb, w_out,
           g_xattn, g_mem, w_qm, w_km, w_vm, w_om, g_ffn, w_grp, b_grp, w_exp, b_exp, w_e1, w_e3, w_e2):
    bsz, s, d = x.shape
    n = bsz * s
    xt = x.reshape(n, d)
    c_qkv = A_WIDTH + 2 * KV_WIDTH
    c_lat = c_qkv + Q_RANK + KV_RANK + IDX_DIM + IDX_HEADS
    w_qkv = w_in[:, :c_qkv].astype(BF16)
    w_lat = jnp.pad(w_in[:, c_qkv:c_lat], ((0, 0), (0, LATENT_WIDTH - (c_lat - c_qkv)))).astype(BF16)
    w_ga = w_in[:, c_lat:c_lat + d].astype(BF16)
    w_gb = w_in[:, c_lat + d:].astype(BF16)

    u = rmsnorm_cast(xt, g_mix)
    qkv = matmul(u, w_qkv, BF16)
    latent = matmul(u, w_lat, F32)
    o_a = swa_attention(qkv, sink_a.astype(F32), _swa_bias(rel_bias[:, :A_HEADS]), bsz, s)

    ckv, ckvt, kidx = latent_norm(latent, g_ckv, g_kidx, bsz, s)
    qbi = rms_matmul_heads(latent, g_cq, jnp.concatenate([w_uq, w_qidx], axis=1).astype(BF16))
    tri = jnp.tril(jnp.ones((P2_CHUNK, P2_CHUNK), BF16))
    o_b = dsa_attention(qbi, latent, kidx, ckv, ckvt, jnp.transpose(w_uk, (1, 2, 0)).astype(BF16),
                        jnp.transpose(w_uv, (1, 0, 2)).astype(BF16), _dsa_bias(rel_bias[:, A_HEADS:]), tri, bsz, s)

    y = gated_merge(u, o_a, o_b, w_ga, w_gb, w_pa.astype(BF16), w_pb.astype(BF16))
    x1 = matmul_residual(y, w_out.astype(BF16), xt)

    n_mem = mem.shape[1]
    um = rmsnorm_cast(mem.reshape(bsz * n_mem, d), g_mem)
    kv = matmul(um, jnp.concatenate([w_km, w_vm], axis=1).astype(BF16), BF16, tm=bsz * n_mem)
    w_router = jnp.pad(jnp.concatenate([w_grp, w_exp], axis=1),
                       ((0, 0), (0, ROUTER_WIDTH - N_GROUPS - N_EXPERTS))).astype(BF16)
    x2, u3, rl = cross_attention(x1, g_xattn, w_qm.astype(BF16), kv.reshape(bsz, n_mem, 2 * MEM_WIDTH),
                                 w_om.astype(BF16), g_ffn, w_router, bsz, s)

    gate, slot_tok, slot_dst, blk_e, n_used, n_real = _moe_routing(rl, b_grp, b_exp)
    ys = moe_ffn(u3, slot_tok, slot_dst, blk_e, n_used, n_real, w_e1, w_e3, w_e2)
    return x2, ys.reshape(EXPERT_TOPK, n, d), gate


def kernel(x, mem, rel_bias, g_mix, w_in, g_cq, w_uq, w_qidx, g_ckv, w_uk, w_uv, g_kidx, sink_a, w_pa, w_pb, w_out,
           g_xattn, g_mem, w_qm, w_km, w_vm, w_om, g_ffn, w_grp, b_grp, w_exp, b_exp, w_e1, w_e3, w_e2, g_final):
    assert g_mix.shape[0] == 1, "one layer"
    bsz, s, d = x.shape
    x2, ys, gate = _layer(x, mem, rel_bias, g_mix[0], w_in[0], g_cq[0], w_uq[0], w_qidx[0], g_ckv[0], w_uk[0],
                              w_uv[0], g_kidx[0], sink_a[0], w_pa[0], w_pb[0], w_out[0], g_xattn[0], g_mem[0],
                              w_qm[0], w_km[0], w_vm[0], w_om[0], g_ffn[0], w_grp[0], b_grp[0], w_exp[0], b_exp[0],
                              w_e1[0], w_e3[0], w_e2[0])
    out = combine_norm(x2, ys, gate, g_final)
    return out.reshape(bsz, s, d)
```

```python
import functools
import math

import numpy as np
import jax
import jax.numpy as jnp
from jax import lax
from jax.experimental import pallas as pl
from jax.experimental.pallas import tpu as pltpu

F32 = jnp.float32
BF16 = jnp.bfloat16

HEAD_DIM = 128
A_HEADS = 16
A_KV_HEADS = 4
A_GROUP = A_HEADS // A_KV_HEADS
WINDOW = 128
A_WIDTH = A_HEADS * HEAD_DIM
KV_WIDTH = A_KV_HEADS * HEAD_DIM
B_HEADS = 16
B_WIDTH = B_HEADS * HEAD_DIM
Q_RANK = 1024
KV_RANK = 512
IDX_HEADS = 16
IDX_DIM = 128
TOPK_KEYS = 256
Q_BLOCK = 128
N_BUCKETS = 32
MAX_DISTANCE = 128
MEM_HEADS = 4
MEM_WIDTH = MEM_HEADS * HEAD_DIM
N_GROUPS = 8
EXPERTS_PER_GROUP = 8
N_EXPERTS = N_GROUPS * EXPERTS_PER_GROUP
EXPERT_FF = 768
EXPERT_TOPK = 2
EPS = 1e-6

LANE = 128
LATENT_WIDTH = 2048
W_IDX_COL = Q_RANK + KV_RANK + IDX_DIM
MOE_ROWS = 640
MOE_PIECES = ((0, 256), (256, 256), (512, 128))
MOE_STEPS = 4
MOE_K = 1024
MOE_F = EXPERT_FF // MOE_STEPS
assert sum(size for _, size in MOE_PIECES) == MOE_ROWS and EXPERT_FF % MOE_STEPS == 0 and MOE_F % 16 == 0
ROUTER_WIDTH = 128
INT_MIN = -(2 ** 31)
KEY_NEG_INF = int(np.array([0xFF800000 ^ 0x7FFFFFFF], np.uint32).view(np.int32)[0])
NEG_BIG = -1e30
NT_DIMS = (((1,), (1,)), ((), ()))


def _params(semantics, vmem_mb):
    return pltpu.CompilerParams(dimension_semantics=semantics, vmem_limit_bytes=int(vmem_mb * 2 ** 20))


def _rms(xf, g):
    return xf * lax.rsqrt(jnp.mean(xf * xf, axis=-1, keepdims=True) + EPS) * g


def _resident(block, index_map):
    return pl.BlockSpec(block, index_map, pipeline_mode=pl.Buffered(1))


def _rmsnorm_kernel(x_ref, g_ref, o_ref):
    o_ref[...] = _rms(x_ref[...], g_ref[...]).astype(o_ref.dtype)


def rmsnorm_cast(x, g, tm=512):
    m, k = x.shape
    return pl.pallas_call(
        _rmsnorm_kernel,
        out_shape=jax.ShapeDtypeStruct((m, k), BF16),
        grid=(m // tm,),
        in_specs=[pl.BlockSpec((tm, k), lambda i: (i, 0)), pl.BlockSpec((1, k), lambda i: (0, 0))],
        out_specs=pl.BlockSpec((tm, k), lambda i: (i, 0)),
        compiler_params=_params(("parallel",), 40),
        name="rmsnorm_cast",
    )(x, g.reshape(1, k))


def _mm_kernel(a_ref, w_ref, o_ref):
    o_ref[...] = jnp.dot(a_ref[...], w_ref[...], preferred_element_type=F32).astype(o_ref.dtype)


def matmul(a, w, out_dtype, tm=1024, tn=512):
    m, k = a.shape
    n = w.shape[1]
    return pl.pallas_call(
        _mm_kernel,
        out_shape=jax.ShapeDtypeStruct((m, n), out_dtype),
        grid=(m // tm, n // tn),
        in_specs=[pl.BlockSpec((tm, k), lambda i, j: (i, 0)), pl.BlockSpec((k, tn), lambda i, j: (0, j))],
        out_specs=pl.BlockSpec((tm, tn), lambda i, j: (i, j)),
        compiler_params=_params(("parallel", "arbitrary"), 48),
        name="matmul",
    )(a, w)


def _mm_res_kernel(a_ref, w_ref, r_ref, o_ref):
    o_ref[...] = r_ref[...] + jnp.dot(a_ref[...], w_ref[...], preferred_element_type=F32)


def matmul_residual(a, w, res, tm=1024, tn=512):
    m, k = a.shape
    n = w.shape[1]
    return pl.pallas_call(
        _mm_res_kernel,
        out_shape=jax.ShapeDtypeStruct((m, n), F32),
        grid=(m // tm, n // tn),
        in_specs=[pl.BlockSpec((tm, k), lambda i, j: (i, 0)), pl.BlockSpec((k, tn), lambda i, j: (0, j)),
                  pl.BlockSpec((tm, tn), lambda i, j: (i, j))],
        out_specs=pl.BlockSpec((tm, tn), lambda i, j: (i, j)),
        compiler_params=_params(("parallel", "arbitrary"), 48),
        name="matmul_residual",
    )(a, w, res)


def _rms_mm_heads_kernel(x_ref, g_ref, w_ref, o_ref, u_ref):
    @pl.when(pl.program_id(1) == 0)
    def _():
        u_ref[...] = _rms(x_ref[...], g_ref[...]).astype(BF16)

    r = jnp.dot(u_ref[...], w_ref[...], preferred_element_type=F32)
    for hh in range(o_ref.shape[0]):
        o_ref[hh] = r[:, hh * LANE:(hh + 1) * LANE].astype(o_ref.dtype)


def rms_matmul_heads(x, g, w, tm=1024, tn=512):
    m = x.shape[0]
    k, n = w.shape
    hb = tn // LANE
    return pl.pallas_call(
        _rms_mm_heads_kernel,
        out_shape=jax.ShapeDtypeStruct((n // LANE, m, LANE), BF16),
        grid=(m // tm, n // tn),
        in_specs=[pl.BlockSpec((tm, k), lambda i, j: (i, 0)), pl.BlockSpec((1, k), lambda i, j: (0, 0)),
                  pl.BlockSpec((k, tn), lambda i, j: (0, j))],
        out_specs=pl.BlockSpec((hb, tm, LANE), lambda i, j: (j, i, 0)),
        scratch_shapes=[pltpu.VMEM((tm, k), BF16)],
        compiler_params=_params(("parallel", "arbitrary"), 40),
        name="rms_matmul_heads",
    )(x, g.reshape(1, k), w)


def _swa_kernel(sink_ref, q_ref, kp_ref, kc_ref, vp_ref, vc_ref, bias_ref, o_ref):
    n = pl.program_id(1)
    kb = jnp.concatenate([kp_ref[...], kc_ref[...]], axis=0)
    vb = jnp.concatenate([vp_ref[...], vc_ref[...]], axis=0)
    col = lax.broadcasted_iota(jnp.int32, (1, 2 * WINDOW), 1)
    no_prev = jnp.where((col < WINDOW) & (n == 0), -jnp.inf, 0.0)
    for kh in range(A_KV_HEADS):
        heads = [kh * A_GROUP + g for g in range(A_GROUP)]
        qs = jnp.concatenate([q_ref[:, h * HEAD_DIM:(h + 1) * HEAD_DIM] for h in heads], axis=0)
        k = kb[:, kh * HEAD_DIM:(kh + 1) * HEAD_DIM]
        v = vb[:, kh * HEAD_DIM:(kh + 1) * HEAD_DIM]
        lg = lax.dot_general(qs, k, NT_DIMS, preferred_element_type=F32) * (HEAD_DIM ** -0.5)
        lg = lg + bias_ref[kh] + no_prev
        sk = jnp.concatenate([jnp.full((WINDOW, 1), sink_ref[h], F32) for h in heads], axis=0)
        mx = jnp.maximum(jnp.max(lg, axis=-1, keepdims=True), sk)
        p = jnp.exp(lg - mx)
        den = jnp.sum(p, axis=-1, keepdims=True) + jnp.exp(sk - mx)
        o = jnp.dot((p * (1.0 / den)).astype(BF16), v, preferred_element_type=F32)
        for g, h in enumerate(heads):
            o_ref[:, h * HEAD_DIM:(h + 1) * HEAD_DIM] = o[g * WINDOW:(g + 1) * WINDOW].astype(o_ref.dtype)


def swa_attention(qkv, sink, bias, bsz, s):
    nb = s // WINDOW
    kcol = A_WIDTH // KV_WIDTH
    cur = lambda b, n: b * nb + n
    prev = lambda b, n: b * nb + jnp.maximum(n - 1, 0)
    return pl.pallas_call(
        _swa_kernel,
        out_shape=jax.ShapeDtypeStruct((bsz * s, A_WIDTH), BF16),
        grid=(bsz, nb),
        in_specs=[
            pl.BlockSpec(memory_space=pltpu.SMEM),
            pl.BlockSpec((WINDOW, A_WIDTH), lambda b, n: (cur(b, n), 0)),
            pl.BlockSpec((WINDOW, KV_WIDTH), lambda b, n: (prev(b, n), kcol)),
            pl.BlockSpec((WINDOW, KV_WIDTH), lambda b, n: (cur(b, n), kcol)),
            pl.BlockSpec((WINDOW, KV_WIDTH), lambda b, n: (prev(b, n), kcol + 1)),
            pl.BlockSpec((WINDOW, KV_WIDTH), lambda b, n: (cur(b, n), kcol + 1)),
            pl.BlockSpec((A_KV_HEADS, A_GROUP * WINDOW, 2 * WINDOW), lambda b, n: (0, 0, 0)),
        ],
        out_specs=pl.BlockSpec((WINDOW, A_WIDTH), lambda b, n: (cur(b, n), 0)),
        compiler_params=_params(("parallel", "arbitrary"), 32),
        name="swa_attention",
    )(sink, qkv, qkv, qkv, qkv, qkv, bias)


def _latent_norm_kernel(x_ref, gkv_ref, gk_ref, ckv_ref, ckvt_ref, kidx_ref):
    c = _rms(x_ref[:, :KV_RANK], gkv_ref[...])
    ckv_ref[...] = c.astype(BF16)
    ckvt_ref[...] = c.T.astype(BF16)
    kidx_ref[...] = _rms(x_ref[:, KV_RANK:KV_RANK + IDX_DIM], gk_ref[...]).astype(BF16)


def latent_norm(latent, g_ckv, g_kidx, bsz, s, tm=512):
    per = s // tm
    half = LATENT_WIDTH // 2
    return pl.pallas_call(
        _latent_norm_kernel,
        out_shape=(jax.ShapeDtypeStruct((bsz, s, KV_RANK), BF16), jax.ShapeDtypeStruct((bsz, KV_RANK, s), BF16),
                   jax.ShapeDtypeStruct((bsz, s, IDX_DIM), BF16)),
        grid=(bsz * per,),
        in_specs=[pl.BlockSpec((tm, half), lambda i: (i, 1)), pl.BlockSpec((1, KV_RANK), lambda i: (0, 0)),
                  pl.BlockSpec((1, IDX_DIM), lambda i: (0, 0))],
        out_specs=(pl.BlockSpec((None, tm, KV_RANK), lambda i: (i // per, i % per, 0)),
                   pl.BlockSpec((None, KV_RANK, tm), lambda i: (i // per, 0, i % per)),
                   pl.BlockSpec((None, tm, IDX_DIM), lambda i: (i // per, i % per, 0))),
        compiler_params=_params(("parallel",), 32),
        name="latent_norm",
    )(latent, g_ckv.reshape(1, KV_RANK), g_kidx.reshape(1, IDX_DIM))


P1_CHUNK = 1024
P2_CHUNK = 256
HEAD_PAIR = 2


def _sortable_key(x):
    bits = lax.bitcast_convert_type(x, jnp.int32)
    return bits ^ ((bits >> 31) & 0x7FFFFFFF)


def _dsa_kernel(qb_ref, qi_ref, w_ref, kidx_ref, ckv_ref, ckvt_ref, wuk_ref, wuv_ref, tab_ref, tri_ref, o_ref,
                key_ref, qlat_ref, acc_ref, m_ref, l_ref):
    n = pl.program_id(1)
    nheads = B_HEADS
    lane_q = lax.broadcasted_iota(jnp.int32, (1, Q_BLOCK), 1)
    t_row = n * Q_BLOCK + lane_q

    for h in range(nheads):
        ql = jnp.dot(qb_ref[h], wuk_ref[h], preferred_element_type=F32) * (HEAD_DIM ** -0.5)
        qlat_ref[h * Q_BLOCK:(h + 1) * Q_BLOCK, :] = ql.astype(BF16)

    w_t = w_ref[...].T

    n_p1 = (n * Q_BLOCK + Q_BLOCK + P1_CHUNK - 1) // P1_CHUNK

    def p1_body(c, carry):
        start = pl.multiple_of(c * P1_CHUNK, P1_CHUNK)
        kc = kidx_ref[pl.ds(start, P1_CHUNK), :]
        score = jnp.zeros((P1_CHUNK, Q_BLOCK), F32)
        hg = 4
        for g0 in range(0, IDX_HEADS, hg):
            q4 = qi_ref[g0:g0 + hg].reshape(hg * Q_BLOCK, IDX_DIM)
            st = lax.dot_general(kc, q4, NT_DIMS, preferred_element_type=F32)
            for j in range(hg):
                h = g0 + j
                score = score + jnp.maximum(st[:, j * Q_BLOCK:(j + 1) * Q_BLOCK], 0.0) * w_t[h:h + 1, :]
        score = score * (IDX_DIM ** -0.5 * IDX_HEADS ** -0.5)
        s_idx = start + lax.broadcasted_iota(jnp.int32, (P1_CHUNK, Q_BLOCK), 0)
        score = jnp.where(s_idx <= t_row, score, -jnp.inf)
        key_ref[pl.ds(start, P1_CHUNK), :] = _sortable_key(score)
        return carry

    lax.fori_loop(0, n_p1, p1_body, 0)

    def count(pred_fn):
        def body(c, cnt8):
            sub = 512
            for j in range(P1_CHUNK // sub):
                blk = key_ref[pl.ds(pl.multiple_of(c * P1_CHUNK + j * sub, sub), sub), :]
                hit = jnp.where(pred_fn(blk), 1, 0).astype(jnp.int32)
                cnt8 = cnt8 + hit.reshape(sub // 8, 8, Q_BLOCK).sum(axis=0)
            return cnt8

        cnt8 = lax.fori_loop(0, n_p1, body, jnp.zeros((8, Q_BLOCK), jnp.int32))
        return cnt8.sum(axis=0, keepdims=True)

    def bit_body(i, thr):
        cand = thr + jnp.left_shift(jnp.int32(1), 31 - i)
        return jnp.where(count(lambda blk: blk >= cand) >= TOPK_KEYS, cand, thr)

    thr = lax.fori_loop(0, 32, bit_body, jnp.full((1, Q_BLOCK), INT_MIN, jnp.int32))
    need = (TOPK_KEYS - count(lambda blk: blk > thr)).astype(F32)

    m_ref[...] = jnp.full(m_ref.shape, NEG_BIG, F32)
    l_ref[...] = jnp.zeros(l_ref.shape, F32)
    acc_ref[...] = jnp.zeros(acc_ref.shape, F32)

    def chunk(c, eq_seen, near):
        start = pl.multiple_of(c * P2_CHUNK, P2_CHUNK)
        u = key_ref[pl.ds(start, P2_CHUNK), :]
        eq = u == thr
        pref = jnp.dot(tri_ref[...], jnp.where(eq, 1.0, 0.0).astype(BF16), preferred_element_type=F32)
        sel = ((u > thr) | (eq & (eq_seen + pref <= need))) & (u != KEY_NEG_INF)
        mask = jnp.where(sel, 0.0, -jnp.inf)
        eq_seen = eq_seen + pref[P2_CHUNK - 1:P2_CHUNK, :]
        kc = ckv_ref[pl.ds(start, P2_CHUNK), :]
        kct = ckvt_ref[:, pl.ds(start, P2_CHUNK)]
        if near:
            halves = []
            for half in range(P2_CHUNK // Q_BLOCK):
                blk = c * (P2_CHUNK // Q_BLOCK) + half
                halves.append(jnp.where(blk == n, tab_ref[1], jnp.where(blk == n - 1, tab_ref[0], 0.0)))
        def pair_logits(hp):
            rows = slice(hp * HEAD_PAIR * Q_BLOCK, (hp + 1) * HEAD_PAIR * Q_BLOCK)
            return lax.dot_general(kc, qlat_ref[rows, :], NT_DIMS, preferred_element_type=F32)

        n_pairs = nheads // HEAD_PAIR
        lg_next = pair_logits(0)
        for hp in range(n_pairs):
            lg = lg_next
            if hp + 1 < n_pairs:
                lg_next = pair_logits(hp + 1)
            ps, alphas = [], []
            for j in range(HEAD_PAIR):
                h = hp * HEAD_PAIR + j
                hs = slice(h * Q_BLOCK, (h + 1) * Q_BLOCK)
                lh = lg[:, j * Q_BLOCK:(j + 1) * Q_BLOCK] + mask
                if near:
                    lh = lh + jnp.concatenate([b[:, hs] for b in halves], axis=0)
                m_old = m_ref[:, hs]
                m_new = jnp.maximum(m_old, jnp.max(lh, axis=0, keepdims=True))
                alpha = jnp.exp(m_old - m_new)
                p = jnp.exp(lh - m_new)
                l_ref[:, hs] = alpha * l_ref[:, hs] + jnp.sum(p, axis=0, keepdims=True)
                m_ref[:, hs] = m_new
                ps.append(p.astype(BF16))
                alphas.append(alpha)
            pv = jnp.dot(kct, jnp.concatenate(ps, axis=1), preferred_element_type=F32)
            acc_ref[hp] = acc_ref[hp] * jnp.concatenate(alphas, axis=1) + pv
        return eq_seen

    per = P2_CHUNK // Q_BLOCK
    c_end = n // per + 1
    c_near = jnp.maximum(n - 1, 0) // per
    eq_seen = lax.fori_loop(0, c_near, functools.partial(chunk, near=False), jnp.zeros((1, Q_BLOCK), F32))
    lax.fori_loop(c_near, c_end, functools.partial(chunk, near=True), eq_seen)

    for h in range(nheads):
        hs = slice(h * Q_BLOCK, (h + 1) * Q_BLOCK)
        in_pair = slice((h % HEAD_PAIR) * Q_BLOCK, (h % HEAD_PAIR + 1) * Q_BLOCK)
        o_lat = (acc_ref[h // HEAD_PAIR, :, in_pair] * (1.0 / l_ref[:, hs])).T
        o = jnp.dot(o_lat.astype(BF16), wuv_ref[h], preferred_element_type=F32)
        o_ref[:, h * HEAD_DIM:(h + 1) * HEAD_DIM] = o.astype(o_ref.dtype)


def dsa_attention(qbi, latent, kidx, ckv, ckvt, wuk_t, wuv_h, tab, tri, bsz, s):
    nb = s // Q_BLOCK
    row = lambda b, n: b * nb + n
    hq = B_HEADS * Q_BLOCK
    return pl.pallas_call(
        _dsa_kernel,
        out_shape=jax.ShapeDtypeStruct((bsz * s, B_WIDTH), BF16),
        grid=(bsz, nb),
        in_specs=[
            pl.BlockSpec((B_HEADS, Q_BLOCK, HEAD_DIM), lambda b, n: (0, row(b, n), 0)),
            pl.BlockSpec((IDX_HEADS, Q_BLOCK, IDX_DIM), lambda b, n: (1, row(b, n), 0)),
            pl.BlockSpec((Q_BLOCK, LANE), lambda b, n: (row(b, n), W_IDX_COL // LANE)),
            _resident((None, s, IDX_DIM), lambda b, n: (b, 0, 0)),
            _resident((None, s, KV_RANK), lambda b, n: (b, 0, 0)),
            _resident((None, KV_RANK, s), lambda b, n: (b, 0, 0)),
            _resident((B_HEADS, HEAD_DIM, KV_RANK), lambda b, n: (0, 0, 0)),
            _resident((B_HEADS, KV_RANK, HEAD_DIM), lambda b, n: (0, 0, 0)),
            _resident((2, Q_BLOCK, hq), lambda b, n: (0, 0, 0)),
            _resident((P2_CHUNK, P2_CHUNK), lambda b, n: (0, 0)),
        ],
        out_specs=pl.BlockSpec((Q_BLOCK, B_WIDTH), lambda b, n: (row(b, n), 0)),
        scratch_shapes=[
            pltpu.VMEM((s, Q_BLOCK), jnp.int32),
            pltpu.VMEM((hq, KV_RANK), BF16),
            pltpu.VMEM((B_HEADS // HEAD_PAIR, KV_RANK, HEAD_PAIR * Q_BLOCK), F32),
            pltpu.VMEM((1, hq), F32),
            pltpu.VMEM((1, hq), F32),
        ],
        compiler_params=_params(("arbitrary", "arbitrary"), 56),
        name="dsa_attention",
    )(qbi, qbi, latent, kidx, ckv, ckvt, wuk_t, wuv_h, tab, tri)


def _merge_kernel(u_ref, oa_ref, ob_ref, wga_ref, wgb_ref, wpa_ref, wpb_ref, y_ref):
    u = u_ref[...]
    ga = jnp.dot(u, wga_ref[...], preferred_element_type=F32)
    gb = jnp.dot(u, wgb_ref[...], preferred_element_type=F32)
    pa = jnp.dot(oa_ref[...], wpa_ref[...], preferred_element_type=F32)
    pb = jnp.dot(ob_ref[...], wpb_ref[...], preferred_element_type=F32)
    y_ref[...] = (jax.nn.sigmoid(ga) * pa + jax.nn.sigmoid(gb) * pb).astype(y_ref.dtype)


def gated_merge(u, o_a, o_b, w_ga, w_gb, w_pa, w_pb, tm=512, tn=512):
    m, d = u.shape
    row = lambda i, j: (i, 0)
    col = lambda i, j: (0, j)
    return pl.pallas_call(
        _merge_kernel,
        out_shape=jax.ShapeDtypeStruct((m, d), BF16),
        grid=(m // tm, d // tn),
        in_specs=[pl.BlockSpec((tm, d), row), pl.BlockSpec((tm, A_WIDTH), row), pl.BlockSpec((tm, B_WIDTH), row),
                  pl.BlockSpec((d, tn), col), pl.BlockSpec((d, tn), col),
                  pl.BlockSpec((A_WIDTH, tn), col), pl.BlockSpec((B_WIDTH, tn), col)],
        out_specs=pl.BlockSpec((tm, tn), lambda i, j: (i, j)),
        compiler_params=_params(("parallel", "arbitrary"), 56),
        name="gated_merge",
    )(u, o_a, o_b, w_ga, w_gb, w_pa, w_pb)


def _xattn_kernel(x_ref, gx_ref, wq_ref, kv_ref, wo_ref, gf_ref, wr_ref, x2_ref, u3_ref, rl_ref):
    x = x_ref[...]
    u = _rms(x, gx_ref[...]).astype(BF16)
    q = jnp.dot(u, wq_ref[...], preferred_element_type=F32).astype(BF16)
    outs = []
    for h in range(MEM_HEADS):
        hs = slice(h * HEAD_DIM, (h + 1) * HEAD_DIM)
        lg = lax.dot_general(q[:, hs], kv_ref[:, hs], NT_DIMS, preferred_element_type=F32) * (HEAD_DIM ** -0.5)
        p = jnp.exp(lg - jnp.max(lg, axis=-1, keepdims=True))
        p = p * (1.0 / jnp.sum(p, axis=-1, keepdims=True))
        v = kv_ref[:, MEM_WIDTH + h * HEAD_DIM:MEM_WIDTH + (h + 1) * HEAD_DIM]
        outs.append(jnp.dot(p.astype(BF16), v, preferred_element_type=F32).astype(BF16))
    o = jnp.concatenate(outs, axis=1)
    x2 = x + jnp.dot(o, wo_ref[...], preferred_element_type=F32)
    x2_ref[...] = x2
    u3 = _rms(x2, gf_ref[...]).astype(BF16)
    u3_ref[...] = _pack_bf16_pairs(u3)
    rl_ref[...] = jnp.dot(u3, wr_ref[...], preferred_element_type=F32)


def _pack_bf16_pairs(a):
    k = a.shape[1] // 2
    lo = lax.bitcast_convert_type(a[:, :k].astype(F32), jnp.int32)
    hi = lax.bitcast_convert_type(a[:, k:].astype(F32), jnp.int32)
    return ((lo >> 16) & 0xFFFF) | (hi & jnp.int32(-65536))


def _unpack_bf16_pairs(w):
    lo = lax.bitcast_convert_type(w << 16, F32).astype(BF16)
    hi = lax.bitcast_convert_type(w & jnp.int32(-65536), F32).astype(BF16)
    return lo, hi


def cross_attention(x, g_x, w_qm, kv, w_om, g_ffn, w_router, bsz, s, tm=256):
    m, d = x.shape
    per = s // tm
    n_mem = kv.shape[1]
    const = lambda i: (0, 0)
    return pl.pallas_call(
        _xattn_kernel,
        out_shape=(jax.ShapeDtypeStruct((m, d), F32), jax.ShapeDtypeStruct((m, d // 2), jnp.int32),
                   jax.ShapeDtypeStruct((m, ROUTER_WIDTH), F32)),
        grid=(m // tm,),
        in_specs=[pl.BlockSpec((tm, d), lambda i: (i, 0)), pl.BlockSpec((1, d), const),
                  _resident((d, MEM_WIDTH), const),
                  pl.BlockSpec((None, n_mem, 2 * MEM_WIDTH), lambda i: (i // per, 0, 0)),
                  _resident((MEM_WIDTH, d), const), pl.BlockSpec((1, d), const),
                  _resident((d, ROUTER_WIDTH), const)],
        out_specs=(pl.BlockSpec((tm, d), lambda i: (i, 0)), pl.BlockSpec((tm, d // 2), lambda i: (i, 0)),
                   pl.BlockSpec((tm, ROUTER_WIDTH), lambda i: (i, 0))),
        compiler_params=_params(("parallel",), 56),
        name="cross_attention",
    )(x, g_x.reshape(1, d), w_qm, kv, w_om, g_ffn.reshape(1, d), w_router)


def _moe_kernel(be_ref, nu_ref, nr_ref, tok_ref, tok_next_ref, dst_ref, u_hbm, w1_ref, w3_ref, w2_ref, y_hbm,
                hbuf, h_scr, y_scr, a_acc, b_acc, w1b, w3b, w2b, gsem, ssem):
    i = pl.program_id(0)
    k = pl.program_id(1)
    n_used = nu_ref[0]
    used = i < n_used
    nr = nr_ref[i]

    def gather_row(tok, r):
        return pltpu.make_async_copy(u_hbm.at[pl.ds(tok, 1), :], hbuf.at[pl.ds(r, 1), :], gsem)

    def scatter_row(r, dst):
        return pltpu.make_async_copy(y_scr.at[pl.ds(r, 1), :], y_hbm.at[pl.ds(dst, 1), :], ssem)

    def wait_all_gathers():
        def body(r, carry):
            gather_row(0, r).wait()
            return carry
        lax.fori_loop(0, MOE_ROWS, body, 0, unroll=8)

    def wait_scatter(count):
        def body(r, carry):
            scatter_row(r, 0).wait()
            return carry
        lax.fori_loop(0, count, body, 0)

    def for_pieces(fn):
        for p, (r0, size) in enumerate(MOE_PIECES):
            if p == 0:
                fn(slice(r0, r0 + size))
            else:
                pl.when(nr > r0)(functools.partial(fn, slice(r0, r0 + size)))

    @pl.when(used & (k == 0))
    def _():
        @pl.when(i == 0)
        def _():
            a_acc[...] = jnp.zeros(a_acc.shape, a_acc.dtype)
            b_acc[...] = jnp.zeros(b_acc.shape, b_acc.dtype)

            def body(r, carry):
                gather_row(tok_ref[0, r], r).start()
                return carry
            lax.fori_loop(0, MOE_ROWS, body, 0)

        wait_all_gathers()
        per_half = MOE_STEPS // 2
        for r0 in range(0, MOE_ROWS, 128):
            lo, hi = _unpack_bf16_pairs(hbuf[r0:r0 + 128, :])
            for kk in range(per_half):
                h_scr[kk, r0:r0 + 128, :] = lo[:, kk * MOE_K:(kk + 1) * MOE_K]
                h_scr[per_half + kk, r0:r0 + 128, :] = hi[:, kk * MOE_K:(kk + 1) * MOE_K]

    @pl.when(used & (k == 1))
    def _():
        for r in range(MOE_ROWS):
            gather_row(tok_next_ref[0, r], r).start()

    @pl.when(used)
    def _():
        w1b[...] = w1_ref[...].astype(BF16)
        w3b[...] = w3_ref[...].astype(BF16)
        w2b[pl.ds(pl.multiple_of(k * MOE_F, MOE_F), MOE_F), :] = w2_ref[...].astype(BF16)

        def up(rows):
            h = h_scr[k, rows, :]
            pa = jnp.dot(h, w1b[...], preferred_element_type=F32)
            pb = jnp.dot(h, w3b[...], preferred_element_type=F32)
            a_acc[rows, :] = jnp.where(k == 0, pa, a_acc[rows, :] + pa)
            b_acc[rows, :] = jnp.where(k == 0, pb, b_acc[rows, :] + pb)

        for_pieces(up)

    @pl.when(used & (k == MOE_STEPS - 1))
    def _():
        @pl.when(i > 0)
        def _():
            wait_scatter(nr_ref[jnp.maximum(i - 1, 0)])

        def down(rows):
            a = a_acc[rows, :]
            g = (a * jax.nn.sigmoid(a) * b_acc[rows, :]).astype(BF16)
            y_scr[rows, :] = jnp.dot(g, w2b[...], preferred_element_type=F32)

        for_pieces(down)

        def body(r, carry):
            scatter_row(r, dst_ref[0, r]).start()
            return carry
        lax.fori_loop(0, nr, body, 0)

        @pl.when(i == n_used - 1)
        def _():
            wait_scatter(nr)
            wait_all_gathers()


def moe_ffn(u_packed, slot_tok, slot_dst, blk_e, n_used, n_real, w1, w3, w2):
    n, dh = u_packed.shape
    d = 2 * dh
    n_blk = slot_tok.shape[0]
    assert d == MOE_STEPS * MOE_K and dh % MOE_K == 0

    def w_map(i, k, be, nu, nr):
        return (be[jnp.minimum(i, nu[0] - 1)], jnp.where(i < nu[0], k, MOE_STEPS - 1), 0)

    def slots(shift):
        return pl.BlockSpec((None, 1, MOE_ROWS), lambda i, k, be, nu, nr: (jnp.minimum(i + shift, nu[0] - 1), 0, 0),
                            memory_space=pltpu.SMEM)

    grid_spec = pltpu.PrefetchScalarGridSpec(
        num_scalar_prefetch=3,
        grid=(n_blk, MOE_STEPS),
        in_specs=[slots(0), slots(1), slots(0), pl.BlockSpec(memory_space=pl.ANY),
                  pl.BlockSpec((None, MOE_K, EXPERT_FF), w_map), pl.BlockSpec((None, MOE_K, EXPERT_FF), w_map),
                  pl.BlockSpec((None, MOE_F, d), w_map)],
        out_specs=pl.BlockSpec(memory_space=pl.ANY),
        scratch_shapes=[pltpu.VMEM((MOE_ROWS, dh), jnp.int32),
                        pltpu.VMEM((MOE_STEPS, MOE_ROWS, MOE_K), BF16),
                        pltpu.VMEM((MOE_ROWS, d), F32),
                        pltpu.VMEM((MOE_ROWS, EXPERT_FF), F32), pltpu.VMEM((MOE_ROWS, EXPERT_FF), F32),
                        pltpu.VMEM((MOE_K, EXPERT_FF), BF16), pltpu.VMEM((MOE_K, EXPERT_FF), BF16),
                        pltpu.VMEM((EXPERT_FF, d), BF16),
                        pltpu.SemaphoreType.DMA(()), pltpu.SemaphoreType.DMA(())],
    )
    return pl.pallas_call(
        _moe_kernel,
        out_shape=jax.ShapeDtypeStruct((EXPERT_TOPK * n, d), F32),
        grid_spec=grid_spec,
        compiler_params=_params(("arbitrary", "arbitrary"), 56),
        name="moe_ffn",
    )(blk_e, n_used, n_real, slot_tok, slot_tok, slot_dst, u_packed, w1, w3, w2)


def _combine_kernel(x_ref, y1_ref, y2_ref, w_ref, g_ref, o_ref):
    w = w_ref[...]
    x3 = x_ref[...] + (y1_ref[...] * w[:, 0:1] + y2_ref[...] * w[:, 1:2])
    o_ref[...] = _rms(x3, g_ref[...])


def combine_norm(x, y, w, g, tm=256):
    m, d = x.shape
    row = lambda i: (i, 0)
    return pl.pallas_call(
        _combine_kernel,
        out_shape=jax.ShapeDtypeStruct((m, d), F32),
        grid=(m // tm,),
        in_specs=[pl.BlockSpec((tm, d), row), pl.BlockSpec((None, tm, d), lambda i: (0, i, 0)),
                  pl.BlockSpec((None, tm, d), lambda i: (1, i, 0)),
                  pl.BlockSpec((tm, EXPERT_TOPK), row), pl.BlockSpec((1, d), lambda i: (0, 0))],
        out_specs=pl.BlockSpec((tm, d), row),
        compiler_params=_params(("parallel",), 48),
        name="combine_norm",
    )(x, y, y, w, g.reshape(1, d))


def _rel_bucket(dist):
    max_exact = N_BUCKETS // 2
    d = jnp.maximum(dist, 0)
    df = jnp.maximum(d, 1).astype(F32)
    large = max_exact + (jnp.log(df / max_exact) / math.log(MAX_DISTANCE / max_exact)
                         * (N_BUCKETS - max_exact)).astype(jnp.int32)
    large = jnp.minimum(large, N_BUCKETS - 1)
    return jnp.where(d < max_exact, d, large)


def _swa_bias(bias_table):
    qi = jnp.arange(WINDOW, dtype=jnp.int32)[:, None]
    sj = jnp.arange(2 * WINDOW, dtype=jnp.int32)[None, :]
    dist = qi + WINDOW - sj
    bias = jnp.transpose(bias_table[_rel_bucket(dist)], (2, 0, 1)).astype(F32)
    bias = jnp.where(((dist >= 0) & (dist < WINDOW))[None], bias, -jnp.inf)
    return bias.reshape(A_KV_HEADS, A_GROUP * WINDOW, 2 * WINDOW)


def _dsa_bias(bias_table):
    kk = jnp.arange(Q_BLOCK, dtype=jnp.int32)[:, None]
    qq = jnp.arange(Q_BLOCK, dtype=jnp.int32)[None, :]
    tabs = []
    for dist in (Q_BLOCK + qq - kk, qq - kk):
        b = bias_table[_rel_bucket(dist)].astype(F32) - bias_table[N_BUCKETS - 1].astype(F32)
        tabs.append(jnp.transpose(b, (0, 2, 1)).reshape(Q_BLOCK, B_HEADS * Q_BLOCK))
    return jnp.stack(tabs)


def _moe_routing(rl, b_grp, b_exp):
    n = rl.shape[0]
    grp_logits = rl[:, :N_GROUPS] + b_grp.astype(F32)
    grp_p = jax.nn.softmax(grp_logits, axis=-1)
    g_top = jnp.argmax(grp_logits, axis=-1).astype(jnp.int32)
    g_gate = jnp.take_along_axis(grp_p, g_top[:, None], axis=1)
    exp_logits = (rl[:, N_GROUPS:N_GROUPS + N_EXPERTS] + b_exp.astype(F32)).reshape(n, N_GROUPS, EXPERTS_PER_GROUP)
    in_grp = jnp.take_along_axis(exp_logits, g_top[:, None, None], axis=1)[:, 0]
    top_v, top_j = lax.top_k(in_grp, EXPERT_TOPK)
    gate = g_gate * jax.nn.softmax(top_v, axis=-1)
    eid = (g_top[:, None] * EXPERTS_PER_GROUP + top_j).reshape(-1).astype(jnp.int32)
    n_assign = n * EXPERT_TOPK
    _, order = lax.sort((eid, jnp.arange(n_assign, dtype=jnp.int32)), num_keys=1, is_stable=True)
    counts = jnp.sum(eid[:, None] == jnp.arange(N_EXPERTS, dtype=jnp.int32)[None, :], axis=0, dtype=jnp.int32)
    padded = ((counts + MOE_ROWS - 1) // MOE_ROWS) * MOE_ROWS
    pad_end = jnp.cumsum(padded)
    pad_start = pad_end - padded
    start = jnp.cumsum(counts) - counts
    n_blk = -(-n_assign // MOE_ROWS) + N_EXPERTS
    blk_start = jnp.arange(n_blk, dtype=jnp.int32) * MOE_ROWS
    blk_e = jnp.minimum(jnp.searchsorted(pad_end, blk_start, side='right'), N_EXPERTS - 1).astype(jnp.int32)
    blk_off = blk_start - pad_start[blk_e]
    n_real = jnp.clip(counts[blk_e] - blk_off, 0, MOE_ROWS).astype(jnp.int32)
    n_used = (pad_end[-1:] // MOE_ROWS).astype(jnp.int32)
    row = jnp.arange(MOE_ROWS, dtype=jnp.int32)[None, :]
    real = row < n_real[:, None]
    assign = order[jnp.clip((start[blk_e] + blk_off)[:, None] + row, 0, n_assign - 1)]
    slot_tok = jnp.where(real, assign // EXPERT_TOPK, 0)
    slot_dst = jnp.where(real, (assign % EXPERT_TOPK) * n + assign // EXPERT_TOPK, 0)
    return (gate, slot_tok.reshape(n_blk, 1, MOE_ROWS), slot_dst.reshape(n_blk, 1, MOE_ROWS), blk_e, n_used, n_real)


def _layer(x, mem, rel_bias, g_mix, w_in, g_cq, w_uq, w_qidx, g_ckv, w_uk, w_uv, g_kidx, sink_a, w_pa, w_pb, w_out,
           g_xattn, g_mem, w_qm, w_km, w_vm, w_om, g_ffn, w_grp, b_grp, w_exp, b_exp, w_e1, w_e3, w_e2):
    bsz, s, d = x.shape
    n = bsz * s
    xt = x.reshape(n, d)
    c_qkv = A_WIDTH + 2 * KV_WIDTH
    c_lat = c_qkv + Q_RANK + KV_RANK + IDX_DIM + IDX_HEADS
    w_qkv = w_in[:, :c_qkv].astype(BF16)
    w_lat = jnp.pad(w_in[:, c_qkv:c_lat], ((0, 0), (0, LATENT_WIDTH - (c_lat - c_qkv)))).astype(BF16)
    w_ga = w_in[:, c_lat:c_lat + d].astype(BF16)
    w_gb = w_in[:, c_lat + d:].astype(BF16)

    u = rmsnorm_cast(xt, g_mix)
    qkv = matmul(u, w_qkv, BF16)
    latent = matmul(u, w_lat, F32)
    o_a = swa_attention(qkv, sink_a.astype(F32), _swa_bias(rel_bias[:, :A_HEADS]), bsz, s)

    ckv, ckvt, kidx = latent_norm(latent, g_ckv, g_kidx, bsz, s)
    qbi = rms_matmul_heads(latent, g_cq, jnp.concatenate([w_uq, w_qidx], axis=1).astype(BF16))
    tri = jnp.tril(jnp.ones((P2_CHUNK, P2_CHUNK), BF16))
    o_b = dsa_attention(qbi, latent, kidx, ckv, ckvt, jnp.transpose(w_uk, (1, 2, 0)).astype(BF16),
                        jnp.transpose(w_uv, (1, 0, 2)).astype(BF16), _dsa_bias(rel_bias[:, A_HEADS:]), tri, bsz, s)

    y = gated_merge(u, o_a, o_b, w_ga, w_gb, w_pa.astype(BF16), w_pb.astype(BF16))
    x1 = matmul_residual(y, w_out.astype(BF16), xt)

    n_mem = mem.shape[1]
    um = rmsnorm_cast(mem.reshape(bsz * n_mem, d), g_mem)
    kv = matmul(um, jnp.concatenate([w_km, w_vm], axis=1).astype(BF16), BF16, tm=bsz * n_mem)
    w_router = jnp.pad(jnp.concatenate([w_grp, w_exp], axis=1),
                       ((0, 0), (0, ROUTER_WIDTH - N_GROUPS - N_EXPERTS))).astype(BF16)
    x2, u3, rl = cross_attention(x1, g_xattn, w_qm.astype(BF16), kv.reshape(bsz, n_mem, 2 * MEM_WIDTH),
                                 w_om.astype(BF16), g_ffn, w_router, bsz, s)

    gate, slot_tok, slot_dst, blk_e, n_used, n_real = _moe_routing(rl, b_grp, b_exp)
    ys = moe_ffn(u3, slot_tok, slot_dst, blk_e, n_used, n_real, w_e1, w_e3, w_e2)
    return x2, ys.reshape(EXPERT_TOPK, n, d), gate


def kernel(x, mem, rel_bias, g_mix, w_in, g_cq, w_uq, w_qidx, g_ckv, w_uk, w_uv, g_kidx, sink_a, w_pa, w_pb, w_out,
           g_xattn, g_mem, w_qm, w_km, w_vm, w_om, g_ffn, w_grp, b_grp, w_exp, b_exp, w_e1, w_e3, w_e2, g_final):
    assert g_mix.shape[0] == 1, "one layer"
    bsz, s, d = x.shape
    x2, ys, gate = _layer(x, mem, rel_bias, g_mix[0], w_in[0], g_cq[0], w_uq[0], w_qidx[0], g_ckv[0], w_uk[0],
                              w_uv[0], g_kidx[0], sink_a[0], w_pa[0], w_pb[0], w_out[0], g_xattn[0], g_mem[0],
                              w_qm[0], w_km[0], w_vm[0], w_om[0], g_ffn[0], w_grp[0], b_grp[0], w_exp[0], b_exp[0],
                              w_e1[0], w_e3[0], w_e2[0])
    out = combine_norm(x2, ys, gate, g_final)
    return out.reshape(bsz, s, d)
```

```python
import functools
import math

import numpy as np
import jax
import jax.numpy as jnp
from jax import lax
from jax.experimental import pallas as pl
from jax.experimental.pallas import tpu as pltpu

F32 = jnp.float32
BF16 = jnp.bfloat16

HEAD_DIM = 128
A_HEADS = 16
A_KV_HEADS = 4
A_GROUP = A_HEADS // A_KV_HEADS
WINDOW = 128
A_WIDTH = A_HEADS * HEAD_DIM
KV_WIDTH = A_KV_HEADS * HEAD_DIM
B_HEADS = 16
B_WIDTH = B_HEADS * HEAD_DIM
Q_RANK = 1024
KV_RANK = 512
IDX_HEADS = 16
IDX_DIM = 128
TOPK_KEYS = 256
Q_BLOCK = 128
N_BUCKETS = 32
MAX_DISTANCE = 128
MEM_HEADS = 4
MEM_WIDTH = MEM_HEADS * HEAD_DIM
N_GROUPS = 8
EXPERTS_PER_GROUP = 8
N_EXPERTS = N_GROUPS * EXPERTS_PER_GROUP
EXPERT_FF = 768
EXPERT_TOPK = 2
EPS = 1e-6

LANE = 128
LATENT_WIDTH = 2048
W_IDX_COL = Q_RANK + KV_RANK + IDX_DIM
MOE_ROWS = 640
MOE_PIECES = ((0, 256), (256, 256), (512, 128))
MOE_STEPS = 4
MOE_K = 1024
MOE_F = EXPERT_FF // MOE_STEPS
assert sum(size for _, size in MOE_PIECES) == MOE_ROWS and EXPERT_FF % MOE_STEPS == 0 and MOE_F % 16 == 0
ROUTER_WIDTH = 128
INT_MIN = -(2 ** 31)
KEY_NEG_INF = int(np.array([0xFF800000 ^ 0x7FFFFFFF], np.uint32).view(np.int32)[0])
NEG_BIG = -1e30
NT_DIMS = (((1,), (1,)), ((), ()))


def _params(semantics, vmem_mb):
    return pltpu.CompilerParams(dimension_semantics=semantics, vmem_limit_bytes=int(vmem_mb * 2 ** 20))


def _rms(xf, g):
    return xf * lax.rsqrt(jnp.mean(xf * xf, axis=-1, keepdims=True) + EPS) * g


def _resident(block, index_map):
    return pl.BlockSpec(block, index_map, pipeline_mode=pl.Buffered(1))


def _rmsnorm_kernel(x_ref, g_ref, o_ref):
    o_ref[...] = _rms(x_ref[...], g_ref[...]).astype(o_ref.dtype)


def rmsnorm_cast(x, g, tm=512):
    m, k = x.shape
    return pl.pallas_call(
        _rmsnorm_kernel,
        out_shape=jax.ShapeDtypeStruct((m, k), BF16),
        grid=(m // tm,),
        in_specs=[pl.BlockSpec((tm, k), lambda i: (i, 0)), pl.BlockSpec((1, k), lambda i: (0, 0))],
        out_specs=pl.BlockSpec((tm, k), lambda i: (i, 0)),
        compiler_params=_params(("parallel",), 40),
        name="rmsnorm_cast",
    )(x, g.reshape(1, k))


def _mm_kernel(a_ref, w_ref, o_ref):
    o_ref[...] = jnp.dot(a_ref[...], w_ref[...], preferred_element_type=F32).astype(o_ref.dtype)


def matmul(a, w, out_dtype, tm=1024, tn=512):
    m, k = a.shape
    n = w.shape[1]
    return pl.pallas_call(
        _mm_kernel,
        out_shape=jax.ShapeDtypeStruct((m, n), out_dtype),
        grid=(m // tm, n // tn),
        in_specs=[pl.BlockSpec((tm, k), lambda i, j: (i, 0)), pl.BlockSpec((k, tn), lambda i, j: (0, j))],
        out_specs=pl.BlockSpec((tm, tn), lambda i, j: (i, j)),
        compiler_params=_params(("parallel", "arbitrary"), 48),
        name="matmul",
    )(a, w)


def _mm_res_kernel(a_ref, w_ref, r_ref, o_ref):
    o_ref[...] = r_ref[...] + jnp.dot(a_ref[...], w_ref[...], preferred_element_type=F32)


def matmul_residual(a, w, res, tm=1024, tn=512):
    m, k = a.shape
    n = w.shape[1]
    return pl.pallas_call(
        _mm_res_kernel,
        out_shape=jax.ShapeDtypeStruct((m, n), F32),
        grid=(m // tm, n // tn),
        in_specs=[pl.BlockSpec((tm, k), lambda i, j: (i, 0)), pl.BlockSpec((k, tn), lambda i, j: (0, j)),
                  pl.BlockSpec((tm, tn), lambda i, j: (i, j))],
        out_specs=pl.BlockSpec((tm, tn), lambda i, j: (i, j)),
        compiler_params=_params(("parallel", "arbitrary"), 48),
        name="matmul_residual",
    )(a, w, res)


def _rms_mm_heads_kernel(x_ref, g_ref, w_ref, o_ref, u_ref):
    @pl.when(pl.program_id(1) == 0)
    def _():
        u_ref[...] = _rms(x_ref[...], g_ref[...]).astype(BF16)

    r = jnp.dot(u_ref[...], w_ref[...], preferred_element_type=F32)
    for hh in range(o_ref.shape[0]):
        o_ref[hh] = r[:, hh * LANE:(hh + 1) * LANE].astype(o_ref.dtype)


def rms_matmul_heads(x, g, w, tm=1024, tn=512):
    m = x.shape[0]
    k, n = w.shape
    hb = tn // LANE
    return pl.pallas_call(
        _rms_mm_heads_kernel,
        out_shape=jax.ShapeDtypeStruct((n // LANE, m, LANE), BF16),
        grid=(m // tm, n // tn),
        in_specs=[pl.BlockSpec((tm, k), lambda i, j: (i, 0)), pl.BlockSpec((1, k), lambda i, j: (0, 0)),
                  pl.BlockSpec((k, tn), lambda i, j: (0, j))],
        out_specs=pl.BlockSpec((hb, tm, LANE), lambda i, j: (j, i, 0)),
        scratch_shapes=[pltpu.VMEM((tm, k), BF16)],
        compiler_params=_params(("parallel", "arbitrary"), 40),
        name="rms_matmul_heads",
    )(x, g.reshape(1, k), w)


def _swa_kernel(sink_ref, q_ref, kp_ref, kc_ref, vp_ref, vc_ref, bias_ref, o_ref):
    n = pl.program_id(1)
    kb = jnp.concatenate([kp_ref[...], kc_ref[...]], axis=0)
    vb = jnp.concatenate([vp_ref[...], vc_ref[...]], axis=0)
    col = lax.broadcasted_iota(jnp.int32, (1, 2 * WINDOW), 1)
    no_prev = jnp.where((col < WINDOW) & (n == 0), -jnp.inf, 0.0)
    for kh in range(A_KV_HEADS):
        heads = [kh * A_GROUP + g for g in range(A_GROUP)]
        qs = jnp.concatenate([q_ref[:, h * HEAD_DIM:(h + 1) * HEAD_DIM] for h in heads], axis=0)
        k = kb[:, kh * HEAD_DIM:(kh + 1) * HEAD_DIM]
        v = vb[:, kh * HEAD_DIM:(kh + 1) * HEAD_DIM]
        lg = lax.dot_general(qs, k, NT_DIMS, preferred_element_type=F32) * (HEAD_DIM ** -0.5)
        lg = lg + bias_ref[kh] + no_prev
        sk = jnp.concatenate([jnp.full((WINDOW, 1), sink_ref[h], F32) for h in heads], axis=0)
        mx = jnp.maximum(jnp.max(lg, axis=-1, keepdims=True), sk)
        p = jnp.exp(lg - mx)
        den = jnp.sum(p, axis=-1, keepdims=True) + jnp.exp(sk - mx)
        o = jnp.dot((p * (1.0 / den)).astype(BF16), v, preferred_element_type=F32)
        for g, h in enumerate(heads):
            o_ref[:, h * HEAD_DIM:(h + 1) * HEAD_DIM] = o[g * WINDOW:(g + 1) * WINDOW].astype(o_ref.dtype)


def swa_attention(qkv, sink, bias, bsz, s):
    nb = s // WINDOW
    kcol = A_WIDTH // KV_WIDTH
    cur = lambda b, n: b * nb + n
    prev = lambda b, n: b * nb + jnp.maximum(n - 1, 0)
    return pl.pallas_call(
        _swa_kernel,
        out_shape=jax.ShapeDtypeStruct((bsz * s, A_WIDTH), BF16),
        grid=(bsz, nb),
        in_specs=[
            pl.BlockSpec(memory_space=pltpu.SMEM),
            pl.BlockSpec((WINDOW, A_WIDTH), lambda b, n: (cur(b, n), 0)),
            pl.BlockSpec((WINDOW, KV_WIDTH), lambda b, n: (prev(b, n), kcol)),
            pl.BlockSpec((WINDOW, KV_WIDTH), lambda b, n: (cur(b, n), kcol)),
            pl.BlockSpec((WINDOW, KV_WIDTH), lambda b, n: (prev(b, n), kcol + 1)),
            pl.BlockSpec((WINDOW, KV_WIDTH), lambda b, n: (cur(b, n), kcol + 1)),
            pl.BlockSpec((A_KV_HEADS, A_GROUP * WINDOW, 2 * WINDOW), lambda b, n: (0, 0, 0)),
        ],
        out_specs=pl.BlockSpec((WINDOW, A_WIDTH), lambda b, n: (cur(b, n), 0)),
        compiler_params=_params(("parallel", "arbitrary"), 32),
        name="swa_attention",
    )(sink, qkv, qkv, qkv, qkv, qkv, bias)


def _latent_norm_kernel(x_ref, gkv_ref, gk_ref, ckv_ref, ckvt_ref, kidx_ref):
    c = _rms(x_ref[:, :KV_RANK], gkv_ref[...])
    ckv_ref[...] = c.astype(BF16)
    ckvt_ref[...] = c.T.astype(BF16)
    kidx_ref[...] = _rms(x_ref[:, KV_RANK:KV_RANK + IDX_DIM], gk_ref[...]).astype(BF16)


def latent_norm(latent, g_ckv, g_kidx, bsz, s, tm=512):
    per = s // tm
    half = LATENT_WIDTH // 2
    return pl.pallas_call(
        _latent_norm_kernel,
        out_shape=(jax.ShapeDtypeStruct((bsz, s, KV_RANK), BF16), jax.ShapeDtypeStruct((bsz, KV_RANK, s), BF16),
                   jax.ShapeDtypeStruct((bsz, s, IDX_DIM), BF16)),
        grid=(bsz * per,),
        in_specs=[pl.BlockSpec((tm, half), lambda i: (i, 1)), pl.BlockSpec((1, KV_RANK), lambda i: (0, 0)),
                  pl.BlockSpec((1, IDX_DIM), lambda i: (0, 0))],
        out_specs=(pl.BlockSpec((None, tm, KV_RANK), lambda i: (i // per, i % per, 0)),
                   pl.BlockSpec((None, KV_RANK, tm), lambda i: (i // per, 0, i % per)),
                   pl.BlockSpec((None, tm, IDX_DIM), lambda i: (i // per, i % per, 0))),
        compiler_params=_params(("parallel",), 32),
        name="latent_norm",
    )(latent, g_ckv.reshape(1, KV_RANK), g_kidx.reshape(1, IDX_DIM))


P1_CHUNK = 1024
P2_CHUNK = 256
HEAD_PAIR = 2


def _sortable_key(x):
    bits = lax.bitcast_convert_type(x, jnp.int32)
    return bits ^ ((bits >> 31) & 0x7FFFFFFF)


def _dsa_kernel(qb_ref, qi_ref, w_ref, kidx_ref, ckv_ref, ckvt_ref, wuk_ref, wuv_ref, tab_ref, tri_ref, o_ref,
                key_ref, qlat_ref, acc_ref, m_ref, l_ref):
    n = pl.program_id(1)
    nheads = B_HEADS
    lane_q = lax.broadcasted_iota(jnp.int32, (1, Q_BLOCK), 1)
    t_row = n * Q_BLOCK + lane_q

    for h in range(nheads):
        ql = jnp.dot(qb_ref[h], wuk_ref[h], preferred_element_type=F32) * (HEAD_DIM ** -0.5)
        qlat_ref[h * Q_BLOCK:(h + 1) * Q_BLOCK, :] = ql.astype(BF16)

    w_t = w_ref[...].T

    n_p1 = (n * Q_BLOCK + Q_BLOCK + P1_CHUNK - 1) // P1_CHUNK

    def p1_body(c, carry):
        start = pl.multiple_of(c * P1_CHUNK, P1_CHUNK)
        kc = kidx_ref[pl.ds(start, P1_CHUNK), :]
        score = jnp.zeros((P1_CHUNK, Q_BLOCK), F32)
        hg = 4
        for g0 in range(0, IDX_HEADS, hg):
            q4 = qi_ref[g0:g0 + hg].reshape(hg * Q_BLOCK, IDX_DIM)
            st = lax.dot_general(kc, q4, NT_DIMS, preferred_element_type=F32)
            for j in range(hg):
                h = g0 + j
                score = score + jnp.maximum(st[:, j * Q_BLOCK:(j + 1) * Q_BLOCK], 0.0) * w_t[h:h + 1, :]
        score = score * (IDX_DIM ** -0.5 * IDX_HEADS ** -0.5)
        s_idx = start + lax.broadcasted_iota(jnp.int32, (P1_CHUNK, Q_BLOCK), 0)
        score = jnp.where(s_idx <= t_row, score, -jnp.inf)
        key_ref[pl.ds(start, P1_CHUNK), :] = _sortable_key(score)
        return carry

    lax.fori_loop(0, n_p1, p1_body, 0)

    def count(pred_fn):
        def body(c, cnt8):
            sub = 512
            for j in range(P1_CHUNK // sub):
                blk = key_ref[pl.ds(pl.multiple_of(c * P1_CHUNK + j * sub, sub), sub), :]
                hit = jnp.where(pred_fn(blk), 1, 0).astype(jnp.int32)
                cnt8 = cnt8 + hit.reshape(sub // 8, 8, Q_BLOCK).sum(axis=0)
            return cnt8

        cnt8 = lax.fori_loop(0, n_p1, body, jnp.zeros((8, Q_BLOCK), jnp.int32))
        return cnt8.sum(axis=0, keepdims=True)

    def bit_body(i, thr):
        cand = thr + jnp.left_shift(jnp.int32(1), 31 - i)
        return jnp.where(count(lambda blk: blk >= cand) >= TOPK_KEYS, cand, thr)

    thr = lax.fori_loop(0, 32, bit_body, jnp.full((1, Q_BLOCK), INT_MIN, jnp.int32))
    need = (TOPK_KEYS - count(lambda blk: blk > thr)).astype(F32)

    m_ref[...] = jnp.full(m_ref.shape, NEG_BIG, F32)
    l_ref[...] = jnp.zeros(l_ref.shape, F32)
    acc_ref[...] = jnp.zeros(acc_ref.shape, F32)

    def chunk(c, eq_seen, near):
        start = pl.multiple_of(c * P2_CHUNK, P2_CHUNK)
        u = key_ref[pl.ds(start, P2_CHUNK), :]
        eq = u == thr
        pref = jnp.dot(tri_ref[...], jnp.where(eq, 1.0, 0.0).astype(BF16), preferred_element_type=F32)
        sel = ((u > thr) | (eq & (eq_seen + pref <= need))) & (u != KEY_NEG_INF)
        mask = jnp.where(sel, 0.0, -jnp.inf)
        eq_seen = eq_seen + pref[P2_CHUNK - 1:P2_CHUNK, :]
        kc = ckv_ref[pl.ds(start, P2_CHUNK), :]
        kct = ckvt_ref[:, pl.ds(start, P2_CHUNK)]
        if near:
            halves = []
            for half in range(P2_CHUNK // Q_BLOCK):
                blk = c * (P2_CHUNK // Q_BLOCK) + half
                halves.append(jnp.where(blk == n, tab_ref[1], jnp.where(blk == n - 1, tab_ref[0], 0.0)))
        for hp in range(nheads // HEAD_PAIR):
            rows = slice(hp * HEAD_PAIR * Q_BLOCK, (hp + 1) * HEAD_PAIR * Q_BLOCK)
            lg = lax.dot_general(kc, qlat_ref[rows, :], NT_DIMS, preferred_element_type=F32)
            ps, alphas = [], []
            for j in range(HEAD_PAIR):
                h = hp * HEAD_PAIR + j
                hs = slice(h * Q_BLOCK, (h + 1) * Q_BLOCK)
                lh = lg[:, j * Q_BLOCK:(j + 1) * Q_BLOCK] + mask
                if near:
                    lh = lh + jnp.concatenate([b[:, hs] for b in halves], axis=0)
                m_old = m_ref[:, hs]
                m_new = jnp.maximum(m_old, jnp.max(lh, axis=0, keepdims=True))
                alpha = jnp.exp(m_old - m_new)
                p = jnp.exp(lh - m_new)
                l_ref[:, hs] = alpha * l_ref[:, hs] + jnp.sum(p, axis=0, keepdims=True)
                m_ref[:, hs] = m_new
                ps.append(p.astype(BF16))
                alphas.append(alpha)
            pv = jnp.dot(kct, jnp.concatenate(ps, axis=1), preferred_element_type=F32)
            acc_ref[hp] = acc_ref[hp] * jnp.concatenate(alphas, axis=1) + pv
        return eq_seen

    per = P2_CHUNK // Q_BLOCK
    c_end = n // per + 1
    c_near = jnp.maximum(n - 1, 0) // per
    eq_seen = lax.fori_loop(0, c_near, functools.partial(chunk, near=False), jnp.zeros((1, Q_BLOCK), F32))
    lax.fori_loop(c_near, c_end, functools.partial(chunk, near=True), eq_seen)

    for h in range(nheads):
        hs = slice(h * Q_BLOCK, (h + 1) * Q_BLOCK)
        in_pair = slice((h % HEAD_PAIR) * Q_BLOCK, (h % HEAD_PAIR + 1) * Q_BLOCK)
        o_lat = (acc_ref[h // HEAD_PAIR, :, in_pair] * (1.0 / l_ref[:, hs])).T
        o = jnp.dot(o_lat.astype(BF16), wuv_ref[h], preferred_element_type=F32)
        o_ref[:, h * HEAD_DIM:(h + 1) * HEAD_DIM] = o.astype(o_ref.dtype)


def dsa_attention(qbi, latent, kidx, ckv, ckvt, wuk_t, wuv_h, tab, tri, bsz, s):
    nb = s // Q_BLOCK
    row = lambda b, n: b * nb + n
    hq = B_HEADS * Q_BLOCK
    return pl.pallas_call(
        _dsa_kernel,
        out_shape=jax.ShapeDtypeStruct((bsz * s, B_WIDTH), BF16),
        grid=(bsz, nb),
        in_specs=[
            pl.BlockSpec((B_HEADS, Q_BLOCK, HEAD_DIM), lambda b, n: (0, row(b, n), 0)),
            pl.BlockSpec((IDX_HEADS, Q_BLOCK, IDX_DIM), lambda b, n: (1, row(b, n), 0)),
            pl.BlockSpec((Q_BLOCK, LANE), lambda b, n: (row(b, n), W_IDX_COL // LANE)),
            _resident((None, s, IDX_DIM), lambda b, n: (b, 0, 0)),
            _resident((None, s, KV_RANK), lambda b, n: (b, 0, 0)),
            _resident((None, KV_RANK, s), lambda b, n: (b, 0, 0)),
            _resident((B_HEADS, HEAD_DIM, KV_RANK), lambda b, n: (0, 0, 0)),
            _resident((B_HEADS, KV_RANK, HEAD_DIM), lambda b, n: (0, 0, 0)),
            _resident((2, Q_BLOCK, hq), lambda b, n: (0, 0, 0)),
            _resident((P2_CHUNK, P2_CHUNK), lambda b, n: (0, 0)),
        ],
        out_specs=pl.BlockSpec((Q_BLOCK, B_WIDTH), lambda b, n: (row(b, n), 0)),
        scratch_shapes=[
            pltpu.VMEM((s, Q_BLOCK), jnp.int32),
            pltpu.VMEM((hq, KV_RANK), BF16),
            pltpu.VMEM((B_HEADS // HEAD_PAIR, KV_RANK, HEAD_PAIR * Q_BLOCK), F32),
            pltpu.VMEM((1, hq), F32),
            pltpu.VMEM((1, hq), F32),
        ],
        compiler_params=_params(("arbitrary", "arbitrary"), 56),
        name="dsa_attention",
    )(qbi, qbi, latent, kidx, ckv, ckvt, wuk_t, wuv_h, tab, tri)


def _merge_kernel(u_ref, oa_ref, ob_ref, wga_ref, wgb_ref, wpa_ref, wpb_ref, y_ref):
    u = u_ref[...]
    ga = jnp.dot(u, wga_ref[...], preferred_element_type=F32)
    gb = jnp.dot(u, wgb_ref[...], preferred_element_type=F32)
    pa = jnp.dot(oa_ref[...], wpa_ref[...], preferred_element_type=F32)
    pb = jnp.dot(ob_ref[...], wpb_ref[...], preferred_element_type=F32)
    y_ref[...] = (jax.nn.sigmoid(ga) * pa + jax.nn.sigmoid(gb) * pb).astype(y_ref.dtype)


def gated_merge(u, o_a, o_b, w_ga, w_gb, w_pa, w_pb, tm=512, tn=512):
    m, d = u.shape
    row = lambda i, j: (i, 0)
    col = lambda i, j: (0, j)
    return pl.pallas_call(
        _merge_kernel,
        out_shape=jax.ShapeDtypeStruct((m, d), BF16),
        grid=(m // tm, d // tn),
        in_specs=[pl.BlockSpec((tm, d), row), pl.BlockSpec((tm, A_WIDTH), row), pl.BlockSpec((tm, B_WIDTH), row),
                  pl.BlockSpec((d, tn), col), pl.BlockSpec((d, tn), col),
                  pl.BlockSpec((A_WIDTH, tn), col), pl.BlockSpec((B_WIDTH, tn), col)],
        out_specs=pl.BlockSpec((tm, tn), lambda i, j: (i, j)),
        compiler_params=_params(("parallel", "arbitrary"), 56),
        name="gated_merge",
    )(u, o_a, o_b, w_ga, w_gb, w_pa, w_pb)


def _xattn_kernel(x_ref, gx_ref, wq_ref, kv_ref, wo_ref, gf_ref, wr_ref, x2_ref, u3_ref, rl_ref):
    x = x_ref[...]
    u = _rms(x, gx_ref[...]).astype(BF16)
    q = jnp.dot(u, wq_ref[...], preferred_element_type=F32).astype(BF16)
    outs = []
    for h in range(MEM_HEADS):
        hs = slice(h * HEAD_DIM, (h + 1) * HEAD_DIM)
        lg = lax.dot_general(q[:, hs], kv_ref[:, hs], NT_DIMS, preferred_element_type=F32) * (HEAD_DIM ** -0.5)
        p = jnp.exp(lg - jnp.max(lg, axis=-1, keepdims=True))
        p = p * (1.0 / jnp.sum(p, axis=-1, keepdims=True))
        v = kv_ref[:, MEM_WIDTH + h * HEAD_DIM:MEM_WIDTH + (h + 1) * HEAD_DIM]
        outs.append(jnp.dot(p.astype(BF16), v, preferred_element_type=F32).astype(BF16))
    o = jnp.concatenate(outs, axis=1)
    x2 = x + jnp.dot(o, wo_ref[...], preferred_element_type=F32)
    x2_ref[...] = x2
    u3 = _rms(x2, gf_ref[...]).astype(BF16)
    u3_ref[...] = _pack_bf16_pairs(u3)
    rl_ref[...] = jnp.dot(u3, wr_ref[...], preferred_element_type=F32)


def _pack_bf16_pairs(a):
    k = a.shape[1] // 2
    lo = lax.bitcast_convert_type(a[:, :k].astype(F32), jnp.int32)
    hi = lax.bitcast_convert_type(a[:, k:].astype(F32), jnp.int32)
    return ((lo >> 16) & 0xFFFF) | (hi & jnp.int32(-65536))


def _unpack_bf16_pairs(w):
    lo = lax.bitcast_convert_type(w << 16, F32).astype(BF16)
    hi = lax.bitcast_convert_type(w & jnp.int32(-65536), F32).astype(BF16)
    return lo, hi


def cross_attention(x, g_x, w_qm, kv, w_om, g_ffn, w_router, bsz, s, tm=256):
    m, d = x.shape
    per = s // tm
    n_mem = kv.shape[1]
    const = lambda i: (0, 0)
    return pl.pallas_call(
        _xattn_kernel,
        out_shape=(jax.ShapeDtypeStruct((m, d), F32), jax.ShapeDtypeStruct((m, d // 2), jnp.int32),
                   jax.ShapeDtypeStruct((m, ROUTER_WIDTH), F32)),
        grid=(m // tm,),
        in_specs=[pl.BlockSpec((tm, d), lambda i: (i, 0)), pl.BlockSpec((1, d), const),
                  _resident((d, MEM_WIDTH), const),
                  pl.BlockSpec((None, n_mem, 2 * MEM_WIDTH), lambda i: (i // per, 0, 0)),
                  _resident((MEM_WIDTH, d), const), pl.BlockSpec((1, d), const),
                  _resident((d, ROUTER_WIDTH), const)],
        out_specs=(pl.BlockSpec((tm, d), lambda i: (i, 0)), pl.BlockSpec((tm, d // 2), lambda i: (i, 0)),
                   pl.BlockSpec((tm, ROUTER_WIDTH), lambda i: (i, 0))),
        compiler_params=_params(("parallel",), 56),
        name="cross_attention",
    )(x, g_x.reshape(1, d), w_qm, kv, w_om, g_ffn.reshape(1, d), w_router)


def _moe_kernel(be_ref, nu_ref, nr_ref, tok_ref, tok_next_ref, dst_ref, u_hbm, w1_ref, w3_ref, w2_ref, y_hbm,
                hbuf, h_scr, y_scr, a_acc, b_acc, w1b, w3b, w2b, gsem, ssem):
    i = pl.program_id(0)
    k = pl.program_id(1)
    n_used = nu_ref[0]
    used = i < n_used
    nr = nr_ref[i]

    def gather_row(tok, r):
        return pltpu.make_async_copy(u_hbm.at[pl.ds(tok, 1), :], hbuf.at[pl.ds(r, 1), :], gsem)

    def scatter_row(r, dst):
        return pltpu.make_async_copy(y_scr.at[pl.ds(r, 1), :], y_hbm.at[pl.ds(dst, 1), :], ssem)

    def wait_all_gathers():
        def body(r, carry):
            gather_row(0, r).wait()
            return carry
        lax.fori_loop(0, MOE_ROWS, body, 0, unroll=8)

    def wait_scatter(count):
        def body(r, carry):
            scatter_row(r, 0).wait()
            return carry
        lax.fori_loop(0, count, body, 0)

    def for_pieces(fn):
        for p, (r0, size) in enumerate(MOE_PIECES):
            if p == 0:
                fn(slice(r0, r0 + size))
            else:
                pl.when(nr > r0)(functools.partial(fn, slice(r0, r0 + size)))

    @pl.when(used & (k == 0))
    def _():
        @pl.when(i == 0)
        def _():
            a_acc[...] = jnp.zeros(a_acc.shape, a_acc.dtype)
            b_acc[...] = jnp.zeros(b_acc.shape, b_acc.dtype)

            def body(r, carry):
                gather_row(tok_ref[0, r], r).start()
                return carry
            lax.fori_loop(0, MOE_ROWS, body, 0)

        wait_all_gathers()
        per_half = MOE_STEPS // 2
        for r0 in range(0, MOE_ROWS, 128):
            lo, hi = _unpack_bf16_pairs(hbuf[r0:r0 + 128, :])
            for kk in range(per_half):
                h_scr[kk, r0:r0 + 128, :] = lo[:, kk * MOE_K:(kk + 1) * MOE_K]
                h_scr[per_half + kk, r0:r0 + 128, :] = hi[:, kk * MOE_K:(kk + 1) * MOE_K]

    @pl.when(used & (k == 1))
    def _():
        for r in range(MOE_ROWS):
            gather_row(tok_next_ref[0, r], r).start()

    @pl.when(used)
    def _():
        w1b[...] = w1_ref[...].astype(BF16)
        w3b[...] = w3_ref[...].astype(BF16)
        w2b[pl.ds(pl.multiple_of(k * MOE_F, MOE_F), MOE_F), :] = w2_ref[...].astype(BF16)

        def up(rows):
            h = h_scr[k, rows, :]
            pa = jnp.dot(h, w1b[...], preferred_element_type=F32)
            pb = jnp.dot(h, w3b[...], preferred_element_type=F32)
            a_acc[rows, :] = jnp.where(k == 0, pa, a_acc[rows, :] + pa)
            b_acc[rows, :] = jnp.where(k == 0, pb, b_acc[rows, :] + pb)

        for_pieces(up)

    @pl.when(used & (k == MOE_STEPS - 1))
    def _():
        @pl.when(i > 0)
        def _():
            wait_scatter(nr_ref[jnp.maximum(i - 1, 0)])

        def down(rows):
            a = a_acc[rows, :]
            g = (a * jax.nn.sigmoid(a) * b_acc[rows, :]).astype(BF16)
            y_scr[rows, :] = jnp.dot(g, w2b[...], preferred_element_type=F32)

        for_pieces(down)

        def body(r, carry):
            scatter_row(r, dst_ref[0, r]).start()
            return carry
        lax.fori_loop(0, nr, body, 0)

        @pl.when(i == n_used - 1)
        def _():
            wait_scatter(nr)
            wait_all_gathers()


def moe_ffn(u_packed, slot_tok, slot_dst, blk_e, n_used, n_real, w1, w3, w2):
    n, dh = u_packed.shape
    d = 2 * dh
    n_blk = slot_tok.shape[0]
    assert d == MOE_STEPS * MOE_K and dh % MOE_K == 0

    def w_map(i, k, be, nu, nr):
        return (be[jnp.minimum(i, nu[0] - 1)], jnp.where(i < nu[0], k, MOE_STEPS - 1), 0)

    def slots(shift):
        return pl.BlockSpec((None, 1, MOE_ROWS), lambda i, k, be, nu, nr: (jnp.minimum(i + shift, nu[0] - 1), 0, 0),
                            memory_space=pltpu.SMEM)

    grid_spec = pltpu.PrefetchScalarGridSpec(
        num_scalar_prefetch=3,
        grid=(n_blk, MOE_STEPS),
        in_specs=[slots(0), slots(1), slots(0), pl.BlockSpec(memory_space=pl.ANY),
                  pl.BlockSpec((None, MOE_K, EXPERT_FF), w_map), pl.BlockSpec((None, MOE_K, EXPERT_FF), w_map),
                  pl.BlockSpec((None, MOE_F, d), w_map)],
        out_specs=pl.BlockSpec(memory_space=pl.ANY),
        scratch_shapes=[pltpu.VMEM((MOE_ROWS, dh), jnp.int32),
                        pltpu.VMEM((MOE_STEPS, MOE_ROWS, MOE_K), BF16),
                        pltpu.VMEM((MOE_ROWS, d), F32),
                        pltpu.VMEM((MOE_ROWS, EXPERT_FF), F32), pltpu.VMEM((MOE_ROWS, EXPERT_FF), F32),
                        pltpu.VMEM((MOE_K, EXPERT_FF), BF16), pltpu.VMEM((MOE_K, EXPERT_FF), BF16),
                        pltpu.VMEM((EXPERT_FF, d), BF16),
                        pltpu.SemaphoreType.DMA(()), pltpu.SemaphoreType.DMA(())],
    )
    return pl.pallas_call(
        _moe_kernel,
        out_shape=jax.ShapeDtypeStruct((EXPERT_TOPK * n, d), F32),
        grid_spec=grid_spec,
        compiler_params=_params(("arbitrary", "arbitrary"), 56),
        name="moe_ffn",
    )(blk_e, n_used, n_real, slot_tok, slot_tok, slot_dst, u_packed, w1, w3, w2)


def _combine_kernel(x_ref, y1_ref, y2_ref, w_ref, g_ref, o_ref):
    w = w_ref[...]
    x3 = x_ref[...] + (y1_ref[...] * w[:, 0:1] + y2_ref[...] * w[:, 1:2])
    o_ref[...] = _rms(x3, g_ref[...])


def combine_norm(x, y, w, g, tm=256):
    m, d = x.shape
    row = lambda i: (i, 0)
    return pl.pallas_call(
        _combine_kernel,
        out_shape=jax.ShapeDtypeStruct((m, d), F32),
        grid=(m // tm,),
        in_specs=[pl.BlockSpec((tm, d), row), pl.BlockSpec((None, tm, d), lambda i: (0, i, 0)),
                  pl.BlockSpec((None, tm, d), lambda i: (1, i, 0)),
                  pl.BlockSpec((tm, EXPERT_TOPK), row), pl.BlockSpec((1, d), lambda i: (0, 0))],
        out_specs=pl.BlockSpec((tm, d), row),
        compiler_params=_params(("parallel",), 48),
        name="combine_norm",
    )(x, y, y, w, g.reshape(1, d))


def _rel_bucket(dist):
    max_exact = N_BUCKETS // 2
    d = jnp.maximum(dist, 0)
    df = jnp.maximum(d, 1).astype(F32)
    large = max_exact + (jnp.log(df / max_exact) / math.log(MAX_DISTANCE / max_exact)
                         * (N_BUCKETS - max_exact)).astype(jnp.int32)
    large = jnp.minimum(large, N_BUCKETS - 1)
    return jnp.where(d < max_exact, d, large)


def _swa_bias(bias_table):
    qi = jnp.arange(WINDOW, dtype=jnp.int32)[:, None]
    sj = jnp.arange(2 * WINDOW, dtype=jnp.int32)[None, :]
    dist = qi + WINDOW - sj
    bias = jnp.transpose(bias_table[_rel_bucket(dist)], (2, 0, 1)).astype(F32)
    bias = jnp.where(((dist >= 0) & (dist < WINDOW))[None], bias, -jnp.inf)
    return bias.reshape(A_KV_HEADS, A_GROUP * WINDOW, 2 * WINDOW)


def _dsa_bias(bias_table):
    kk = jnp.arange(Q_BLOCK, dtype=jnp.int32)[:, None]
    qq = jnp.arange(Q_BLOCK, dtype=jnp.int32)[None, :]
    tabs = []
    for dist in (Q_BLOCK + qq - kk, qq - kk):
        b = bias_table[_rel_bucket(dist)].astype(F32) - bias_table[N_BUCKETS - 1].astype(F32)
        tabs.append(jnp.transpose(b, (0, 2, 1)).reshape(Q_BLOCK, B_HEADS * Q_BLOCK))
    return jnp.stack(tabs)


def _moe_routing(rl, b_grp, b_exp):
    n = rl.shape[0]
    grp_logits = rl[:, :N_GROUPS] + b_grp.astype(F32)
    grp_p = jax.nn.softmax(grp_logits, axis=-1)
    g_top = jnp.argmax(grp_logits, axis=-1).astype(jnp.int32)
    g_gate = jnp.take_along_axis(grp_p, g_top[:, None], axis=1)
    exp_logits = (rl[:, N_GROUPS:N_GROUPS + N_EXPERTS] + b_exp.astype(F32)).reshape(n, N_GROUPS, EXPERTS_PER_GROUP)
    in_grp = jnp.take_along_axis(exp_logits, g_top[:, None, None], axis=1)[:, 0]
    top_v, top_j = lax.top_k(in_grp, EXPERT_TOPK)
    gate = g_gate * jax.nn.softmax(top_v, axis=-1)
    eid = (g_top[:, None] * EXPERTS_PER_GROUP + top_j).reshape(-1).astype(jnp.int32)
    n_assign = n * EXPERT_TOPK
    _, order = lax.sort((eid, jnp.arange(n_assign, dtype=jnp.int32)), num_keys=1, is_stable=True)
    counts = jnp.sum(eid[:, None] == jnp.arange(N_EXPERTS, dtype=jnp.int32)[None, :], axis=0, dtype=jnp.int32)
    padded = ((counts + MOE_ROWS - 1) // MOE_ROWS) * MOE_ROWS
    pad_end = jnp.cumsum(padded)
    pad_start = pad_end - padded
    start = jnp.cumsum(counts) - counts
    n_blk = -(-n_assign // MOE_ROWS) + N_EXPERTS
    blk_start = jnp.arange(n_blk, dtype=jnp.int32) * MOE_ROWS
    blk_e = jnp.minimum(jnp.searchsorted(pad_end, blk_start, side='right'), N_EXPERTS - 1).astype(jnp.int32)
    blk_off = blk_start - pad_start[blk_e]
    n_real = jnp.clip(counts[blk_e] - blk_off, 0, MOE_ROWS).astype(jnp.int32)
    n_used = (pad_end[-1:] // MOE_ROWS).astype(jnp.int32)
    row = jnp.arange(MOE_ROWS, dtype=jnp.int32)[None, :]
    real = row < n_real[:, None]
    first = jnp.clip(start[blk_e] + blk_off, 0, n_assign)
    order_pad = jnp.concatenate([order, jnp.zeros((MOE_ROWS,), jnp.int32)])
    assign = jax.vmap(lambda f: lax.dynamic_slice(order_pad, (f,), (MOE_ROWS,)))(first)
    slot_tok = jnp.where(real, assign // EXPERT_TOPK, 0)
    slot_dst = jnp.where(real, (assign % EXPERT_TOPK) * n + assign // EXPERT_TOPK, 0)
    return (gate, slot_tok.reshape(n_blk, 1, MOE_ROWS), slot_dst.reshape(n_blk, 1, MOE_ROWS), blk_e, n_used, n_real)


def _layer(x, mem, rel_bias, g_mix, w_in, g_cq, w_uq, w_qidx, g_ckv, w_uk, w_uv, g_kidx, sink_a, w_pa, w_pb, w_out,
           g_xattn, g_mem, w_qm, w_km, w_vm, w_om, g_ffn, w_grp, b_grp, w_exp, b_exp, w_e1, w_e3, w_e2):
    bsz, s, d = x.shape
    n = bsz * s
    xt = x.reshape(n, d)
    c_qkv = A_WIDTH + 2 * KV_WIDTH
    c_lat = c_qkv + Q_RANK + KV_RANK + IDX_DIM + IDX_HEADS
    w_qkv = w_in[:, :c_qkv].astype(BF16)
    w_lat = jnp.pad(w_in[:, c_qkv:c_lat], ((0, 0), (0, LATENT_WIDTH - (c_lat - c_qkv)))).astype(BF16)
    w_ga = w_in[:, c_lat:c_lat + d].astype(BF16)
    w_gb = w_in[:, c_lat + d:].astype(BF16)

    u = rmsnorm_cast(xt, g_mix)
    qkv = matmul(u, w_qkv, BF16)
    latent = matmul(u, w_lat, F32)
    o_a = swa_attention(qkv, sink_a.astype(F32), _swa_bias(rel_bias[:, :A_HEADS]), bsz, s)

    ckv, ckvt, kidx = latent_norm(latent, g_ckv, g_kidx, bsz, s)
    qbi = rms_matmul_heads(latent, g_cq, jnp.concatenate([w_uq, w_qidx], axis=1).astype(BF16))
    tri = jnp.tril(jnp.ones((P2_CHUNK, P2_CHUNK), BF16))
    o_b = dsa_attention(qbi, latent, kidx, ckv, ckvt, jnp.transpose(w_uk, (1, 2, 0)).astype(BF16),
                        jnp.transpose(w_uv, (1, 0, 2)).astype(BF16), _dsa_bias(rel_bias[:, A_HEADS:]), tri, bsz, s)

    y = gated_merge(u, o_a, o_b, w_ga, w_gb, w_pa.astype(BF16), w_pb.astype(BF16))
    x1 = matmul_residual(y, w_out.astype(BF16), xt)

    n_mem = mem.shape[1]
    um = rmsnorm_cast(mem.reshape(bsz * n_mem, d), g_mem)
    kv = matmul(um, jnp.concatenate([w_km, w_vm], axis=1).astype(BF16), BF16, tm=bsz * n_mem)
    w_router = jnp.pad(jnp.concatenate([w_grp, w_exp], axis=1),
                       ((0, 0), (0, ROUTER_WIDTH - N_GROUPS - N_EXPERTS))).astype(BF16)
    x2, u3, rl = cross_attention(x1, g_xattn, w_qm.astype(BF16), kv.reshape(bsz, n_mem, 2 * MEM_WIDTH),
                                 w_om.astype(BF16), g_ffn, w_router, bsz, s)

    gate, slot_tok, slot_dst, blk_e, n_used, n_real = _moe_routing(rl, b_grp, b_exp)
    ys = moe_ffn(u3, slot_tok, slot_dst, blk_e, n_used, n_real, w_e1, w_e3, w_e2)
    return x2, ys.reshape(EXPERT_TOPK, n, d), gate


def kernel(x, mem, rel_bias, g_mix, w_in, g_cq, w_uq, w_qidx, g_ckv, w_uk, w_uv, g_kidx, sink_a, w_pa, w_pb, w_out,
           g_xattn, g_mem, w_qm, w_km, w_vm, w_om, g_ffn, w_grp, b_grp, w_exp, b_exp, w_e1, w_e3, w_e2, g_final):
    assert g_mix.shape[0] == 1, "one layer"
    bsz, s, d = x.shape
    x2, ys, gate = _layer(x, mem, rel_bias, g_mix[0], w_in[0], g_cq[0], w_uq[0], w_qidx[0], g_ckv[0], w_uk[0],
                              w_uv[0], g_kidx[0], sink_a[0], w_pa[0], w_pb[0], w_out[0], g_xattn[0], g_mem[0],
                              w_qm[0], w_km[0], w_vm[0], w_om[0], g_ffn[0], w_grp[0], b_grp[0], w_exp[0], b_exp[0],
                              w_e1[0], w_e3[0], w_e2[0])
    out = combine_norm(x2, ys, gate, g_final)
    return out.reshape(bsz, s, d)
```
